```python
import jax, jax.numpy as jnp
from jax import lax
import numpy as np

D_MODEL = 1024
BATCH = 8
SEQ = 8192
DEPTH = 2
DEC_BATCH = 16
DEC_SEQ = 2048
PAST_LEN = 128

HEAD_DIM = 64
A_Q_HEADS = 16
A_KV_HEADS = 4
A_RADIUS = 128
B_GROUPS = ((128, 1), (512, 4), (2048, 16))
B_Q_PER_GROUP = 6
B_KV_PER_GROUP = 2
B_Q_HEADS = B_Q_PER_GROUP * len(B_GROUPS)
B_KV_HEADS = B_KV_PER_GROUP * len(B_GROUPS)
FFN_HIDDEN = -(-8 * D_MODEL // (3 * 256)) * 256
PLE_DIM = 256
N_A_LAYERS = (DEPTH + 1) // 2
N_B_LAYERS = DEPTH // 2
EPS = 1e-6
NEG_INF = -1e30

kernel_name = "hybrid_window_dilated_encoder"


def alibi_slopes(n):
    return 2.0 ** (-8.0 * jnp.arange(1, n + 1, dtype=jnp.float32) / n)


def rmsnorm(x, g):
    xf = x.astype(jnp.float32)
    y = xf * lax.rsqrt(jnp.mean(xf * xf, axis=-1, keepdims=True) + EPS) * g.astype(jnp.float32)
    return y.astype(x.dtype)


def banded_attention(q, k, v, slopes, radius, stride, sink):
    n, length, hq, hd = q.shape
    hkv = k.shape[2]
    grp = hq // hkv
    blk = radius
    nb = -(-length // blk)
    lp = nb * blk
    q = jnp.pad(q, ((0, 0), (0, lp - length), (0, 0), (0, 0)))
    kv_pad = ((0, 0), (blk, lp - length + blk), (0, 0), (0, 0))
    k = jnp.pad(k, kv_pad)
    v = jnp.pad(v, kv_pad)
    qb = q.reshape(n, nb, blk, hkv, grp, hd)

    def windows(t):
        tb = t.reshape(n, nb + 2, blk, hkv, hd)
        return jnp.concatenate([tb[:, :-2], tb[:, 1:-1], tb[:, 2:]], axis=2)

    kw, vw = windows(k), windows(v)
    logits = jnp.einsum('nbqkgd,nbskd->nbkgqs', qb, kw, preferred_element_type=jnp.float32)
    qpos = jnp.arange(lp).reshape(nb, blk)
    kpos = (jnp.arange(nb)[:, None] - 1) * blk + jnp.arange(3 * blk)[None, :]
    dist = jnp.abs(qpos[:, :, None] - kpos[:, None, :])
    valid = (dist <= radius) & (kpos[:, None, :] >= 0) & (kpos[:, None, :] < length)
    alibi = -slopes.astype(jnp.float32).reshape(hkv, grp)[None, :, :, None, None] * \
        (stride * dist).astype(jnp.float32)[:, None, None]
    logits = jnp.where(valid[:, None, None], logits + alibi, NEG_INF)
    m = logits.max(axis=-1)
    if sink is not None:
        s = sink.astype(jnp.float32).reshape(hkv, grp)[:, :, None]
        m = jnp.maximum(m, s)
    p = jnp.exp(logits - m[..., None])
    denom = p.sum(axis=-1)
    if sink is not None:
        denom = denom + jnp.exp(s - m)
    o = jnp.einsum('nbkgqs,nbskd->nbqkgd', p.astype(v.dtype), vw, preferred_element_type=jnp.float32)
    o = o / denom.transpose(0, 1, 4, 2, 3)[..., None]
    o = o.reshape(n, lp, hq, hd)[:, :length].astype(q.dtype)
    lse = (m + jnp.log(denom)).transpose(0, 1, 4, 2, 3).reshape(n, lp, hq)[:, :length]
    return o, lse


def split_qkv(h, wqkv, hq, hkv, q_gain, k_gain):
    b, s, _ = h.shape
    qkv = h @ wqkv
    q = qkv[..., :hq * HEAD_DIM].reshape(b, s, hq, HEAD_DIM)
    k = qkv[..., hq * HEAD_DIM:(hq + hkv) * HEAD_DIM].reshape(b, s, hkv, HEAD_DIM)
    v = qkv[..., (hq + hkv) * HEAD_DIM:].reshape(b, s, hkv, HEAD_DIM)
    q = rmsnorm(q, q_gain) * jnp.asarray(HEAD_DIM ** -0.5, dtype=h.dtype)
    k = rmsnorm(k, k_gain)
    return q, k, v


def mixer_window_gqa(h, wqkv, wo, q_gain, k_gain, sink):
    b, s, _ = h.shape
    q, k, v = split_qkv(h, wqkv, A_Q_HEADS, A_KV_HEADS, q_gain, k_gain)
    o, _ = banded_attention(q, k, v, alibi_slopes(A_Q_HEADS), A_RADIUS, 1, sink)
    return o.reshape(b, s, A_Q_HEADS * HEAD_DIM) @ wo


def mixer_dilated(h, wqkv, wo, q_gain, k_gain):
    b, s, _ = h.shape
    q, k, v = split_qkv(h, wqkv, B_Q_HEADS, B_KV_HEADS, q_gain, k_gain)
    slopes = alibi_slopes(B_Q_HEADS)
    outs, lses = [], []
    for g, (window, dil) in enumerate(B_GROUPS):
        radius = window // (2 * dil)
        qs = slice(g * B_Q_PER_GROUP, (g + 1) * B_Q_PER_GROUP)
        ks = slice(g * B_KV_PER_GROUP, (g + 1) * B_KV_PER_GROUP)

        def to_res(t):
            hh = t.shape[2]
            return t.reshape(b, s // dil, dil, hh, HEAD_DIM).transpose(0, 2, 1, 3, 4).reshape(b * dil, s // dil, hh, HEAD_DIM)

        o, lse = banded_attention(to_res(q[:, :, qs]), to_res(k[:, :, ks]), to_res(v[:, :, ks]),
                                  slopes[qs], radius, dil, None)
        o = o.reshape(b, dil, s // dil, B_Q_PER_GROUP, HEAD_DIM).transpose(0, 2, 1, 3, 4).reshape(b, s, B_Q_PER_GROUP, HEAD_DIM)
        lse = lse.reshape(b, dil, s // dil, B_Q_PER_GROUP).transpose(0, 2, 1, 3).reshape(b, s, B_Q_PER_GROUP)
        outs.append(o)
        lses.append(lse)
    alpha = jax.nn.softmax(jnp.stack(lses, axis=0), axis=0)
    o = jnp.concatenate([outs[g] * alpha[g][..., None].astype(h.dtype) for g in range(len(B_GROUPS))], axis=2)
    return o.reshape(b, s, B_Q_HEADS * HEAD_DIM) @ wo


def swiglu(h, w_gate, w_up, w_down):
    return (jax.nn.silu(h @ w_gate) * (h @ w_up)) @ w_down


def encoder_trunk(x, p, norm_mix, norm_ffn, norm_ple,
                  a_wqkv, a_wo, a_q_gain, a_k_gain, a_sink,
                  b_wqkv, b_wo, b_q_gain, b_k_gain,
                  ffn_w_gate, ffn_w_up, ffn_w_down, ple_w_gate, ple_w_proj):
    for i in range(DEPTH):
        hn = rmsnorm(x, norm_mix[i])
        j = i // 2
        if i % 2 == 0:
            mix = mixer_window_gqa(hn, a_wqkv[j], a_wo[j], a_q_gain[j], a_k_gain[j], a_sink[j])
        else:
            mix = mixer_dilated(hn, b_wqkv[j], b_wo[j], b_q_gain[j], b_k_gain[j])
        x = x + mix
        x = x + swiglu(rmsnorm(x, norm_ffn[i]), ffn_w_gate[i], ffn_w_up[i], ffn_w_down[i])
        gate = jax.nn.sigmoid(rmsnorm(x, norm_ple[i]) @ ple_w_gate[i])
        x = x + gate * (p[i] @ ple_w_proj[i])
    return x


def setup_inputs(seed: int = 0) -> dict:
    key = jax.random.key(seed)
    ks = jax.random.split(key, 24)
    f32 = jnp.float32

    def w(k, shape, fan_in):
        return jax.random.normal(k, shape, f32) * fan_in ** -0.5

    def gain(k, shape):
        return 1.0 + 0.02 * jax.random.normal(k, shape, f32)

    a_cols = (A_Q_HEADS + 2 * A_KV_HEADS) * HEAD_DIM
    b_cols = (B_Q_HEADS + 2 * B_KV_HEADS) * HEAD_DIM
    return {
        "x_prompt": jax.random.normal(ks[0], (BATCH, SEQ, D_MODEL), f32),
        "x_sample": jax.random.normal(ks[1], (DEC_BATCH, DEC_SEQ, D_MODEL), f32),
        "p_prompt": jax.random.normal(ks[2], (DEPTH, BATCH, SEQ, PLE_DIM), f32),
        "p_sample": jax.random.normal(ks[3], (DEPTH, DEC_BATCH, DEC_SEQ, PLE_DIM), f32),
        "norm_mix": gain(ks[4], (DEPTH, D_MODEL)),
        "norm_ffn": gain(ks[5], (DEPTH, D_MODEL)),
        "norm_ple": gain(ks[6], (DEPTH, D_MODEL)),
        "a_wqkv": w(ks[7], (N_A_LAYERS, D_MODEL, a_cols), D_MODEL),
        "a_wo": w(ks[8], (N_A_LAYERS, A_Q_HEADS * HEAD_DIM, D_MODEL), A_Q_HEADS * HEAD_DIM),
        "a_q_gain": gain(ks[9], (N_A_LAYERS, HEAD_DIM)),
        "a_k_gain": gain(ks[10], (N_A_LAYERS, HEAD_DIM)),
        "a_sink": 0.5 * jax.random.normal(ks[11], (N_A_LAYERS, A_Q_HEADS), f32),
        "b_wqkv": w(ks[12], (N_B_LAYERS, D_MODEL, b_cols), D_MODEL),
        "b_wo": w(ks[13], (N_B_LAYERS, B_Q_HEADS * HEAD_DIM, D_MODEL), B_Q_HEADS * HEAD_DIM),
        "b_q_gain": gain(ks[14], (N_B_LAYERS, HEAD_DIM)),
        "b_k_gain": gain(ks[15], (N_B_LAYERS, HEAD_DIM)),
        "ffn_w_gate": w(ks[16], (DEPTH, D_MODEL, FFN_HIDDEN), D_MODEL),
        "ffn_w_up": w(ks[17], (DEPTH, D_MODEL, FFN_HIDDEN), D_MODEL),
        "ffn_w_down": w(ks[18], (DEPTH, FFN_HIDDEN, D_MODEL), FFN_HIDDEN),
        "ple_w_gate": w(ks[19], (DEPTH, D_MODEL, D_MODEL), D_MODEL),
        "ple_w_proj": w(ks[20], (DEPTH, PLE_DIM, D_MODEL), PLE_DIM),
    }


def reference(x_prompt, x_sample, p_prompt, p_sample, norm_mix, norm_ffn, norm_ple,
              a_wqkv, a_wo, a_q_gain, a_k_gain, a_sink,
              b_wqkv, b_wo, b_q_gain, b_k_gain,
              ffn_w_gate, ffn_w_up, ffn_w_down, ple_w_gate, ple_w_proj):
    y_prompt = encoder_trunk(x_prompt, p_prompt, norm_mix, norm_ffn, norm_ple,
                             a_wqkv, a_wo, a_q_gain, a_k_gain, a_sink,
                             b_wqkv, b_wo, b_q_gain, b_k_gain,
                             ffn_w_gate, ffn_w_up, ffn_w_down, ple_w_gate, ple_w_proj)
    y_sample = encoder_trunk(x_sample, p_sample, norm_mix, norm_ffn, norm_ple,
                             a_wqkv, a_wo, a_q_gain, a_k_gain, a_sink,
                             b_wqkv, b_wo, b_q_gain, b_k_gain,
                             ffn_w_gate, ffn_w_up, ffn_w_down, ple_w_gate, ple_w_proj)
    return (y_prompt, y_sample)
```

```python
import functools

import numpy as np
import jax
import jax.numpy as jnp
from jax import lax
from jax.experimental import pallas as pl
from jax.experimental.pallas import tpu as pltpu

D_MODEL = 1024
HEAD_DIM = 64
PLE_DIM = 256
FFN_HIDDEN = 2816
EPS = 1e-6

A_Q_HEADS = 16
A_KV_HEADS = 4
A_RADIUS = 128
B_GROUPS = ((128, 1), (512, 4), (2048, 16))
B_Q_PER_GROUP = 6
B_KV_PER_GROUP = 2
B_Q_HEADS = B_Q_PER_GROUP * len(B_GROUPS)
B_KV_HEADS = B_KV_PER_GROUP * len(B_GROUPS)

LANES = 128
MXU_DIM = 256
SUB_ROWS = 128
MASK_BIAS = -1e32
VMEM_LIMIT = 56 * 1024 * 1024

F32 = jnp.float32
BF16 = jnp.bfloat16


def _const_spec(shape):
    nd = len(shape)
    return pl.BlockSpec(shape, lambda *_: (0,) * nd, pipeline_mode=pl.Buffered(1))


def _rmsnorm_rows(x, gain):
    ms = jnp.mean(x * x, axis=-1, keepdims=True)
    return x * lax.rsqrt(ms + EPS) * gain


def _head_sumsq(t, bd):
    width = t.shape[1]
    sq = (t * t).astype(BF16)
    parts = []
    c = 0
    while c < width:
        w = min(MXU_DIM, width - c)
        parts.append(jnp.dot(sq[:, c:c + w], bd[:w, :w], preferred_element_type=F32))
        c += w
    return parts[0] if len(parts) == 1 else jnp.concatenate(parts, axis=1)


def _qkv_kernel(x_ref, gn_ref, w_ref, gq_ref, gk_ref, bd_ref, q_ref, k_ref, v_ref, *, cq, ck):
    h = _rmsnorm_rows(x_ref[...], gn_ref[...]).astype(BF16)
    qkv = jnp.dot(h, w_ref[...], preferred_element_type=F32)
    bd = bd_ref[...]
    q = qkv[:, :cq]
    k = qkv[:, cq:cq + ck]
    inv_hd = 1.0 / HEAD_DIM
    q = q * lax.rsqrt(_head_sumsq(q, bd) * inv_hd + EPS) * gq_ref[...]
    k = k * lax.rsqrt(_head_sumsq(k, bd) * inv_hd + EPS) * gk_ref[...]
    q_ref[...] = q.astype(BF16)
    k_ref[...] = k.astype(BF16)
    v_ref[...] = qkv[:, cq + ck:].astype(BF16)


def _qkv_call(x2d, gn, w, gq, gk, bd, *, cq, ck, tm):
    t = x2d.shape[0]
    c = cq + 2 * ck
    row = lambda r: (r, 0)
    return pl.pallas_call(
        functools.partial(_qkv_kernel, cq=cq, ck=ck),
        grid=(t // tm,),
        in_specs=[
            pl.BlockSpec((tm, D_MODEL), row),
            _const_spec((1, D_MODEL)),
            _const_spec((D_MODEL, c)),
            _const_spec((1, cq)),
            _const_spec((1, ck)),
            _const_spec((MXU_DIM, MXU_DIM)),
        ],
        out_specs=[
            pl.BlockSpec((tm, cq), row),
            pl.BlockSpec((tm, ck), row),
            pl.BlockSpec((tm, ck), row),
        ],
        out_shape=[
            jax.ShapeDtypeStruct((t, cq), BF16),
            jax.ShapeDtypeStruct((t, ck), BF16),
            jax.ShapeDtypeStruct((t, ck), BF16),
        ],
        compiler_params=pltpu.CompilerParams(
            dimension_semantics=("arbitrary",), vmem_limit_bytes=VMEM_LIMIT),
        name="qkv",
    )(x2d, gn, w, gq, gk, bd)


def _attn_kernel(*refs, radius, bq, grp, has_sink, has_lse):
    nsub = bq // SUB_ROWS
    wp = SUB_ROWS + 2 * radius
    it = iter(refs)
    slope_ref = next(it)
    sink_ref = next(it) if has_sink else None
    q_ref = next(it)
    kp_ref, kc_ref, kn_ref, vp_ref, vc_ref, vn_ref = (next(it) for _ in range(6))
    tab_refs = [next(it) for _ in range(nsub)]
    o_ref = next(it)
    lse_ref = next(it) if has_lse else None

    slab = pl.program_id(2)
    kwin = jnp.concatenate([kp_ref[0], kc_ref[0], kn_ref[0]], axis=0)
    vwin = jnp.concatenate([vp_ref[0], vc_ref[0], vn_ref[0]], axis=0)
    klane = lax.broadcasted_iota(jnp.int32, kwin.shape, 1)
    kzero = jnp.zeros_like(kwin)
    k_half = [jnp.where(klane < HEAD_DIM, kwin, kzero),
              jnp.where(klane >= HEAD_DIM, kwin, kzero)]
    olane = lax.broadcasted_iota(jnp.int32, (SUB_ROWS, LANES), 1)

    for t in range(nsub):
        r0 = t * SUB_ROWS
        dist = tab_refs[t][0]
        vsub = vwin[r0:r0 + wp]
        lse_acc = jnp.zeros((SUB_ROWS, LANES), F32) if has_lse else None
        for a in range(grp):
            qs = q_ref[0, r0:r0 + SUB_ROWS, a * LANES:(a + 1) * LANES]
            halves = []
            for par in range(2):
                hidx = slab * (2 * grp) + par * grp + a
                sc = lax.dot_general(qs, k_half[par][r0:r0 + wp],
                                     (((1,), (1,)), ((), ())),
                                     preferred_element_type=F32)
                sc = sc + slope_ref[hidx] * dist
                m = jnp.max(sc, axis=-1, keepdims=True)
                if has_sink:
                    sink = sink_ref[hidx]
                    m = jnp.maximum(m, sink)
                p = jnp.exp(sc - m)
                den = jnp.sum(p, axis=-1, keepdims=True)
                if has_sink:
                    den = den + jnp.exp(sink - m)
                pv = jnp.dot(p.astype(BF16), vsub, preferred_element_type=F32)
                halves.append(pv / den)
                if has_lse:
                    lse_acc = jnp.where(olane == par * grp + a, m + jnp.log(den), lse_acc)
            o = jnp.where(olane < HEAD_DIM, halves[0], halves[1])
            o_ref[0, r0:r0 + SUB_ROWS, a * LANES:(a + 1) * LANES] = o.astype(BF16)
        if has_lse:
            lse_ref[0, r0:r0 + SUB_ROWS, :] = lse_acc


def _bias_tables(radius, stride):
    wp = SUB_ROWS + 2 * radius
    i = np.arange(SUB_ROWS)[:, None]
    j = np.arange(wp)[None, :]
    rel = j - radius - i
    band = np.abs(rel) <= radius
    tabs = []
    for code in range(4):
        valid = band
        if code & 1:
            valid = valid & (j >= radius)
        if code & 2:
            valid = valid & (j < SUB_ROWS + radius)
        tabs.append(np.where(valid, -float(stride) * np.abs(rel), MASK_BIAS))
    return jnp.asarray(np.stack(tabs), dtype=F32)


def _attn_call(q, k, v, slopes, sink, *, radius, stride, grp, want_lse):
    n, length, _ = q.shape
    slabs = k.shape[2] // LANES
    bq = 256 if length % 256 == 0 else SUB_ROWS
    assert length % bq == 0 and bq % radius == 0
    nsub = bq // SUB_ROWS
    wp = SUB_ROWS + 2 * radius
    nblk = length // bq
    halo_per_blk = bq // radius
    n_halo = length // radius
    total_sub = length // SUB_ROWS
    tables = _bias_tables(radius, stride)

    def prev_map(b, i, s):
        return (b, jnp.maximum(i * halo_per_blk - 1, 0), s)

    def next_map(b, i, s):
        return (b, jnp.minimum((i + 1) * halo_per_blk, n_halo - 1), s)

    cur_map = lambda b, i, s: (b, i, s)

    def tab_map(t):
        def f(b, i, s):
            u = i * nsub + t
            code = (u == 0).astype(jnp.int32) + 2 * (u == total_sub - 1).astype(jnp.int32)
            return (code, 0, 0)
        return f

    smem = pl.BlockSpec(memory_space=pltpu.SMEM)
    in_specs = [smem]
    args = [slopes]
    if sink is not None:
        in_specs.append(smem)
        args.append(sink)
    in_specs.append(pl.BlockSpec((1, bq, grp * LANES), cur_map))
    args.append(q)
    for arr in (k, v):
        in_specs += [pl.BlockSpec((1, radius, LANES), prev_map),
                     pl.BlockSpec((1, bq, LANES), cur_map),
                     pl.BlockSpec((1, radius, LANES), next_map)]
        args += [arr, arr, arr]
    for t in range(nsub):
        in_specs.append(pl.BlockSpec((1, SUB_ROWS, wp), tab_map(t)))
        args.append(tables)

    out_specs = [pl.BlockSpec((1, bq, grp * LANES), cur_map)]
    out_shape = [jax.ShapeDtypeStruct(q.shape, BF16)]
    if want_lse:
        assert slabs == 1
        out_specs.append(pl.BlockSpec((1, bq, LANES), lambda b, i, s: (b, i, 0)))
        out_shape.append(jax.ShapeDtypeStruct((n, length, LANES), F32))

    return pl.pallas_call(
        functools.partial(_attn_kernel, radius=radius, bq=bq, grp=grp,
                          has_sink=sink is not None, has_lse=want_lse),
        grid=(n, nblk, slabs),
        in_specs=in_specs,
        out_specs=out_specs,
        out_shape=out_shape,
        compiler_params=pltpu.CompilerParams(
            dimension_semantics=("arbitrary", "arbitrary", "arbitrary"),
            vmem_limit_bytes=VMEM_LIMIT),
        name="attn",
    )(*args)


def _ffn_chunks():
    chunks, c = [], 0
    while c < FFN_HIDDEN:
        w = min(2 * MXU_DIM, FFN_HIDDEN - c)
        chunks.append((c, w))
        c += w
    return chunks


def _post_kernel(*refs, n_groups):
    it = iter(refs)
    x_ref = next(it)
    o_refs = [next(it) for _ in range(n_groups)]
    lse_refs = [next(it) for _ in range(n_groups)] if n_groups > 1 else []
    p_ref = next(it)
    wo_ref, nf_ref, wg_ref, wu_ref, wd_ref, np_ref, pg_ref, pp_ref = (next(it) for _ in range(8))
    out_ref = next(it)

    if n_groups == 1:
        o = o_refs[0][...]
    else:
        lses = [r[...] for r in lse_refs]
        mx = functools.reduce(jnp.maximum, lses)
        es = [jnp.exp(l - mx) for l in lses]
        inv = 1.0 / functools.reduce(lambda a, b: a + b, es)
        tm = lses[0].shape[0]
        lane = lax.broadcasted_iota(jnp.int32, (tm, LANES), 1)
        grp = B_Q_PER_GROUP // B_KV_PER_GROUP
        pieces = []
        for g in range(n_groups):
            alpha = es[g] * inv
            for a in range(grp):
                slab = o_refs[g][:, a * LANES:(a + 1) * LANES].astype(F32)
                mult = jnp.where(lane < HEAD_DIM, alpha[:, a:a + 1], alpha[:, grp + a:grp + a + 1])
                pieces.append((slab * mult).astype(BF16))
        o = jnp.concatenate(pieces, axis=1)

    x1 = x_ref[...] + jnp.dot(o, wo_ref[...], preferred_element_type=F32)
    h = _rmsnorm_rows(x1, nf_ref[...]).astype(BF16)
    acc = None
    for c, w in _ffn_chunks():
        g = jnp.dot(h, wg_ref[:, c:c + w], preferred_element_type=F32)
        u = jnp.dot(h, wu_ref[:, c:c + w], preferred_element_type=F32)
        act = (g * jax.nn.sigmoid(g) * u).astype(BF16)
        d = jnp.dot(act, wd_ref[c:c + w, :], preferred_element_type=F32)
        acc = d if acc is None else acc + d
    x2 = x1 + acc
    hp = _rmsnorm_rows(x2, np_ref[...]).astype(BF16)
    gate = jax.nn.sigmoid(jnp.dot(hp, pg_ref[...], preferred_element_type=F32))
    proj = jnp.dot(p_ref[...].astype(BF16), pp_ref[...], preferred_element_type=F32)
    out_ref[...] = x2 + gate * proj


def _post_call(x2d, os_, lses, p3d, layer, wo, nf, wg, wu, wd, npl, pg, pp, *, tm):
    t = x2d.shape[0]
    row = lambda r: (r, 0)
    n_groups = len(os_)
    in_specs = [pl.BlockSpec((tm, D_MODEL), row)]
    in_specs += [pl.BlockSpec((tm, o.shape[1]), row) for o in os_]
    in_specs += [pl.BlockSpec((tm, LANES), row) for _ in lses]
    in_specs.append(pl.BlockSpec((None, tm, PLE_DIM), lambda r: (layer, r, 0)))
    weights = [wo, nf, wg, wu, wd, npl, pg, pp]
    in_specs += [_const_spec(w.shape) for w in weights]
    return pl.pallas_call(
        functools.partial(_post_kernel, n_groups=n_groups),
        grid=(t // tm,),
        in_specs=in_specs,
        out_specs=pl.BlockSpec((tm, D_MODEL), row),
        out_shape=jax.ShapeDtypeStruct((t, D_MODEL), F32),
        compiler_params=pltpu.CompilerParams(
            dimension_semantics=("arbitrary",), vmem_limit_bytes=VMEM_LIMIT),
        name="post",
    )(x2d, *os_, *lses, p3d, *weights)


def _alibi_slopes(n):
    return 2.0 ** (-8.0 * np.arange(1, n + 1, dtype=np.float64) / n)


def _a_head_order():
    grp = A_Q_HEADS // A_KV_HEADS
    order = []
    for s in range(A_KV_HEADS // 2):
        for a in range(grp):
            for par in range(2):
                order.append((2 * s + par) * grp + a)
    return order


def _b_head_order():
    grp = B_Q_PER_GROUP // B_KV_PER_GROUP
    order = []
    for g in range(len(B_GROUPS)):
        for a in range(grp):
            for par in range(2):
                order.append(g * B_Q_PER_GROUP + par * grp + a)
    return order


def _slab_param_order(n_slabs, grp):
    idx = []
    for s in range(n_slabs):
        for par in range(2):
            for a in range(grp):
                idx.append((s * grp + a) * 2 + par)
    return idx


def _block_diag_ones():
    r = np.arange(MXU_DIM) // HEAD_DIM
    return jnp.asarray((r[:, None] == r[None, :]).astype(np.float32), dtype=BF16)


def _prep_qkv_weights(wqkv, wo, q_gain, k_gain, hq, hkv, order):
    cq, ck = hq * HEAD_DIM, hkv * HEAD_DIM
    order = np.asarray(order)
    wq = wqkv[:, :cq].reshape(D_MODEL, hq, HEAD_DIM)[:, order].reshape(D_MODEL, cq)
    w = jnp.concatenate([wq, wqkv[:, cq:]], axis=1).astype(BF16)
    wo_p = wo.reshape(hq, HEAD_DIM, D_MODEL)[order].reshape(cq, D_MODEL).astype(BF16)
    gq = jnp.tile(q_gain * (HEAD_DIM ** -0.5), hq).reshape(1, cq)
    gk = jnp.tile(k_gain, hkv).reshape(1, ck)
    return w, wo_p, gq, gk


def _to_residues(t, dil):
    b, s, c = t.shape
    if dil == 1:
        return t
    return t.reshape(b, s // dil, dil, c).transpose(0, 2, 1, 3).reshape(b * dil, s // dil, c)


def _from_residues(t, b, dil):
    if dil == 1:
        return t
    bd, l, c = t.shape
    return t.reshape(b, dil, l, c).transpose(0, 2, 1, 3).reshape(b, l * dil, c)


def _trunk(x, p, norm_mix, norm_ffn, norm_ple, a_prep, b_prep, ffn_prep, *, tm):
    b, s, _ = x.shape
    t = b * s
    x2d = x.reshape(t, D_MODEL)
    p3d = p.reshape(p.shape[0], t, PLE_DIM)
    bd = _block_diag_ones()
    a_grp = A_Q_HEADS // A_KV_HEADS
    b_grp = B_Q_PER_GROUP // B_KV_PER_GROUP

    w, wo_p, gq, gk, slopes, sink = a_prep
    cq, ck = A_Q_HEADS * HEAD_DIM, A_KV_HEADS * HEAD_DIM
    q, k, v = _qkv_call(x2d, norm_mix[0].reshape(1, -1), w, gq, gk, bd, cq=cq, ck=ck, tm=tm)
    o = _attn_call(q.reshape(b, s, cq), k.reshape(b, s, ck), v.reshape(b, s, ck), slopes, sink,
                   radius=A_RADIUS, stride=1, grp=a_grp, want_lse=False)[0]
    x2d = _post_call(x2d, [o.reshape(t, cq)], [], p3d, 0, wo_p, *ffn_prep[0], tm=tm)

    w, wo_p, gq, gk, slopes = b_prep
    cq, ck = B_Q_HEADS * HEAD_DIM, B_KV_HEADS * HEAD_DIM
    q, k, v = _qkv_call(x2d, norm_mix[1].reshape(1, -1), w, gq, gk, bd, cq=cq, ck=ck, tm=tm)
    q, k, v = q.reshape(b, s, cq), k.reshape(b, s, ck), v.reshape(b, s, ck)
    gq_w = b_grp * LANES
    os_, lses = [], []
    for g, (window, dil) in enumerate(B_GROUPS):
        radius = window // (2 * dil)
        og, lg = _attn_call(
            _to_residues(q[:, :, g * gq_w:(g + 1) * gq_w], dil),
            _to_residues(k[:, :, g * LANES:(g + 1) * LANES], dil),
            _to_residues(v[:, :, g * LANES:(g + 1) * LANES], dil),
            slopes[g], None, radius=radius, stride=dil, grp=b_grp, want_lse=True)
        os_.append(_from_residues(og, b, dil).reshape(t, gq_w))
        lses.append(_from_residues(lg, b, dil).reshape(t, LANES))
    x2d = _post_call(x2d, os_, lses, p3d, 1, wo_p, *ffn_prep[1], tm=tm)
    return x2d.reshape(b, s, D_MODEL)


def kernel(x_prompt, x_sample, p_prompt, p_sample, norm_mix, norm_ffn, norm_ple,
           a_wqkv, a_wo, a_q_gain, a_k_gain, a_sink,
           b_wqkv, b_wo, b_q_gain, b_k_gain,
           ffn_w_gate, ffn_w_up, ffn_w_down, ple_w_gate, ple_w_proj):
    a_order, b_order = _a_head_order(), _b_head_order()
    a_grp = A_Q_HEADS // A_KV_HEADS
    b_grp = B_Q_PER_GROUP // B_KV_PER_GROUP

    a_par = np.asarray(a_order)[_slab_param_order(A_KV_HEADS // 2, a_grp)]
    a_slopes = jnp.asarray(_alibi_slopes(A_Q_HEADS)[a_par], dtype=F32)
    a_prep = _prep_qkv_weights(a_wqkv[0], a_wo[0], a_q_gain[0], a_k_gain[0],
                               A_Q_HEADS, A_KV_HEADS, a_order) + (a_slopes, a_sink[0][a_par])

    b_par = np.asarray(b_order)[_slab_param_order(len(B_GROUPS), b_grp)].reshape(len(B_GROUPS), -1)
    b_slopes = [jnp.asarray(_alibi_slopes(B_Q_HEADS)[b_par[g]], dtype=F32) for g in range(len(B_GROUPS))]
    b_prep = _prep_qkv_weights(b_wqkv[0], b_wo[0], b_q_gain[0], b_k_gain[0],
                               B_Q_HEADS, B_KV_HEADS, b_order) + (b_slopes,)

    ffn_prep = []
    for i in range(norm_ffn.shape[0]):
        ffn_prep.append((norm_ffn[i].reshape(1, -1), ffn_w_gate[i].astype(BF16), ffn_w_up[i].astype(BF16),
                         ffn_w_down[i].astype(BF16), norm_ple[i].reshape(1, -1),
                         ple_w_gate[i].astype(BF16), ple_w_proj[i].astype(BF16)))

    run = functools.partial(_trunk, norm_mix=norm_mix, norm_ffn=norm_ffn, norm_ple=norm_ple,
                            a_prep=a_prep, b_prep=b_prep, ffn_prep=ffn_prep, tm=512)
    return (run(x_prompt, p_prompt), run(x_sample, p_sample))
```

```python
import functools
import math

import numpy as np
import jax
import jax.numpy as jnp
from jax import lax
from jax.experimental import pallas as pl
from jax.experimental.pallas import tpu as pltpu

D_MODEL = 1024
HEAD_DIM = 64
PLE_DIM = 256
FFN_HIDDEN = 2816
EPS = 1e-6

A_Q_HEADS = 16
A_KV_HEADS = 4
A_RADIUS = 128
B_GROUPS = ((128, 1), (512, 4), (2048, 16))
B_Q_PER_GROUP = 6
B_KV_PER_GROUP = 2
B_Q_HEADS = B_Q_PER_GROUP * len(B_GROUPS)
B_KV_HEADS = B_KV_PER_GROUP * len(B_GROUPS)
A_GRP = A_Q_HEADS // A_KV_HEADS
B_GRP = B_Q_PER_GROUP // B_KV_PER_GROUP

LANES = 128
MXU_DIM = 256
SUB_ROWS = 128
MASK_BIAS = -1e32
VMEM_LIMIT = 56 * 1024 * 1024
LOG2E = math.log2(math.e)
LN2 = math.log(2.0)

F32 = jnp.float32
BF16 = jnp.bfloat16


def _const_spec(shape):
    nd = len(shape)
    return pl.BlockSpec(shape, lambda *_: (0,) * nd, pipeline_mode=pl.Buffered(1))


def _rmsnorm_rows(x, gain):
    ms = jnp.mean(x * x, axis=-1, keepdims=True)
    return x * lax.rsqrt(ms + EPS) * gain


def _head_sumsq(t, bd):
    width = t.shape[1]
    sq = (t * t).astype(BF16)
    parts = []
    c = 0
    while c < width:
        w = min(MXU_DIM, width - c)
        parts.append(jnp.dot(sq[:, c:c + w], bd[:w, :w], preferred_element_type=F32))
        c += w
    return parts[0] if len(parts) == 1 else jnp.concatenate(parts, axis=1)


def _qkv_kernel(x_ref, gn_ref, w_ref, gq_ref, gk_ref, bd_ref, *rest, cq, ck, outs):
    out_refs = rest[:len(outs)]
    stage_ref = rest[len(outs)] if len(rest) > len(outs) else None
    tm = x_ref.shape[0]
    h = _rmsnorm_rows(x_ref[...], gn_ref[...]).astype(BF16)
    qkv = jnp.dot(h, w_ref[...], preferred_element_type=F32)
    bd = bd_ref[...]
    q = qkv[:, :cq]
    k = qkv[:, cq:cq + ck]
    inv_hd = 1.0 / HEAD_DIM
    q = q * lax.rsqrt(_head_sumsq(q, bd) * inv_hd + EPS) * gq_ref[...]
    k = k * lax.rsqrt(_head_sumsq(k, bd) * inv_hd + EPS) * gk_ref[...]
    full = jnp.concatenate([q, k, qkv[:, cq + ck:]], axis=1)
    if stage_ref is not None:
        for j in range(stage_ref.shape[0]):
            stage_ref[j] = full[:, j * LANES:(j + 1) * LANES]
    for (off, w, dil), o_ref in zip(outs, out_refs):
        if dil == 1:
            o_ref[...] = full[:, off:off + w].astype(BF16)
        else:
            rows = tm // dil
            for r in range(dil):
                for j in range(w // LANES):
                    piece = stage_ref[off // LANES + j, pl.ds(r, rows, stride=dil), :]
                    o_ref[:, r * w + j * LANES:r * w + (j + 1) * LANES] = piece.astype(BF16)


def _qkv_call(x2d, gn, w, gq, gk, bd, *, cq, ck, outs, tm):
    t = x2d.shape[0]
    c = cq + 2 * ck
    row = lambda r: (r, 0)
    staged = any(d > 1 for _, _, d in outs)
    return pl.pallas_call(
        functools.partial(_qkv_kernel, cq=cq, ck=ck, outs=tuple(outs)),
        grid=(t // tm,),
        in_specs=[
            pl.BlockSpec((tm, D_MODEL), row),
            _const_spec((1, D_MODEL)),
            _const_spec((D_MODEL, c)),
            _const_spec((1, cq)),
            _const_spec((1, ck)),
            _const_spec((MXU_DIM, MXU_DIM)),
        ],
        out_specs=[pl.BlockSpec((tm // d, d * w_), row) for _, w_, d in outs],
        out_shape=[jax.ShapeDtypeStruct((t // d, d * w_), BF16) for _, w_, d in outs],
        scratch_shapes=[pltpu.VMEM((c // LANES, tm, LANES), F32)] if staged else [],
        compiler_params=pltpu.CompilerParams(
            dimension_semantics=("arbitrary",), vmem_limit_bytes=VMEM_LIMIT),
        name="qkv",
    )(x2d, gn, w, gq, gk, bd)


def _attn_kernel(*refs, radius, bq, grp, has_sink, has_lse, heads_by_col):
    nsub = bq // SUB_ROWS
    wp = SUB_ROWS + 2 * radius
    g_rows = grp * SUB_ROWS
    it = iter(refs)
    sink_ref = next(it) if has_sink else None
    q_ref = next(it)
    kp_ref, kc_ref, kn_ref, vp_ref, vc_ref, vn_ref = (next(it) for _ in range(6))
    tab_refs = [next(it) for _ in range(nsub)]
    o_ref = next(it)
    lse_ref = next(it) if has_lse else None

    head0 = pl.program_id(0) * (2 * grp) if heads_by_col else 0
    kwin = jnp.concatenate([kp_ref[0], kc_ref[0], kn_ref[0]], axis=0)
    vwin = jnp.concatenate([vp_ref[0], vc_ref[0], vn_ref[0]], axis=0)
    klane = lax.broadcasted_iota(jnp.int32, kwin.shape, 1)
    kzero = jnp.zeros_like(kwin)
    k_half = [jnp.where(klane < HEAD_DIM, kwin, kzero),
              jnp.where(klane >= HEAD_DIM, kwin, kzero)]
    olane = lax.broadcasted_iota(jnp.int32, (SUB_ROWS, LANES), 1)

    for t in range(nsub):
        r0 = t * SUB_ROWS
        qst = jnp.concatenate(
            [q_ref[0, r0:r0 + SUB_ROWS, a * LANES:(a + 1) * LANES] for a in range(grp)], axis=0)
        probs, dens, maxes = [], [], []
        for par in range(2):
            sc = lax.dot_general(qst, k_half[par][r0:r0 + wp], (((1,), (1,)), ((), ())),
                                 preferred_element_type=F32)
            for a in range(grp):
                rows = slice(a * SUB_ROWS, (a + 1) * SUB_ROWS)
                sa = sc[rows] + tab_refs[t][par * g_rows + a * SUB_ROWS:par * g_rows + (a + 1) * SUB_ROWS, :]
                m = jnp.max(sa, axis=-1, keepdims=True)
                if has_sink:
                    sink = sink_ref[head0 + par * grp + a]
                    m = jnp.maximum(m, sink)
                p = jnp.exp2(sa - m)
                den = jnp.sum(p, axis=-1, keepdims=True)
                if has_sink:
                    den = den + jnp.exp2(sink - m)
                probs.append(p.astype(BF16))
                dens.append(den)
                maxes.append(m)
        pv = jnp.dot(jnp.concatenate(probs, axis=0), vwin[r0:r0 + wp], preferred_element_type=F32)
        lse_acc = jnp.zeros((SUB_ROWS, LANES), F32) if has_lse else None
        for a in range(grp):
            halves = []
            for par in range(2):
                h = par * grp + a
                halves.append(pv[h * SUB_ROWS:(h + 1) * SUB_ROWS] / dens[h])
                if has_lse:
                    lse_acc = jnp.where(olane == h, LN2 * maxes[h] + jnp.log(dens[h]), lse_acc)
            o_ref[0, r0:r0 + SUB_ROWS, a * LANES:(a + 1) * LANES] = jnp.where(
                olane < HEAD_DIM, halves[0], halves[1]).astype(BF16)
        if has_lse:
            lse_ref[0, r0:r0 + SUB_ROWS, :] = lse_acc


def _bias_tables(radius, stride, slopes, grp):
    wp = SUB_ROWS + 2 * radius
    i = np.arange(SUB_ROWS)[:, None]
    j = np.arange(wp)[None, :]
    rel = j - radius - i
    band = np.abs(rel) <= radius
    tabs = []
    for code in range(4):
        valid = band
        if code & 1:
            valid = valid & (j >= radius)
        if code & 2:
            valid = valid & (j < SUB_ROWS + radius)
        tabs.append(np.where(valid, -float(stride) * LOG2E * np.abs(rel), MASK_BIAS))
    base = jnp.asarray(np.stack(tabs), dtype=F32)
    slabs = slopes.shape[0]
    full = base[:, None, None] * slopes[None, :, :, None, None]
    return full.reshape(4, slabs, 2 * grp * SUB_ROWS, wp)


def _attn_call(q, k, v, slopes, sink, *, radius, stride, grp, want_lse):
    n, length, _ = q.shape
    cols = k.shape[2] // LANES
    heads_by_col = sink is not None
    bq = 256 if length % 256 == 0 else SUB_ROWS
    assert length % bq == 0 and bq % radius == 0
    nsub = bq // SUB_ROWS
    wp = SUB_ROWS + 2 * radius
    g_rows = 2 * grp * SUB_ROWS
    nblk = length // bq
    halo_per_blk = bq // radius
    n_halo = length // radius
    total_sub = length // SUB_ROWS
    tables = _bias_tables(radius, stride, slopes, grp)

    def prev_map(c, b, i):
        return (b, jnp.maximum(i * halo_per_blk - 1, 0), c)

    def next_map(c, b, i):
        return (b, jnp.minimum((i + 1) * halo_per_blk, n_halo - 1), c)

    cur_map = lambda c, b, i: (b, i, c)

    def tab_map(t):
        def f(c, b, i):
            u = i * nsub + t
            code = (u == 0).astype(jnp.int32) + 2 * (u == total_sub - 1).astype(jnp.int32)
            return (code, c if heads_by_col else 0, 0, 0)
        return f

    in_specs, args = [], []
    if sink is not None:
        in_specs.append(pl.BlockSpec(memory_space=pltpu.SMEM))
        args.append(sink)
    in_specs.append(pl.BlockSpec((1, bq, grp * LANES), cur_map))
    args.append(q)
    for arr in (k, v):
        in_specs += [pl.BlockSpec((1, radius, LANES), prev_map),
                     pl.BlockSpec((1, bq, LANES), cur_map),
                     pl.BlockSpec((1, radius, LANES), next_map)]
        args += [arr, arr, arr]
    for t in range(nsub):
        in_specs.append(pl.BlockSpec((None, None, g_rows, wp), tab_map(t)))
        args.append(tables)

    out_specs = [pl.BlockSpec((1, bq, grp * LANES), cur_map)]
    out_shape = [jax.ShapeDtypeStruct(q.shape, BF16)]
    if want_lse:
        out_specs.append(pl.BlockSpec((1, bq, LANES), cur_map))
        out_shape.append(jax.ShapeDtypeStruct((n, length, cols * LANES), F32))

    return pl.pallas_call(
        functools.partial(_attn_kernel, radius=radius, bq=bq, grp=grp, has_sink=sink is not None,
                          has_lse=want_lse, heads_by_col=heads_by_col),
        grid=(cols, n, nblk),
        in_specs=in_specs,
        out_specs=out_specs,
        out_shape=out_shape,
        compiler_params=pltpu.CompilerParams(
            dimension_semantics=("arbitrary", "arbitrary", "arbitrary"),
            vmem_limit_bytes=VMEM_LIMIT),
        name="attn",
    )(*args)


def _ffn_chunks():
    chunks, c = [], 0
    while c < FFN_HIDDEN:
        w = min(2 * MXU_DIM, FFN_HIDDEN - c)
        chunks.append((c, w))
        c += w
    return chunks


def _gather_residues(src_ref, stage_ref, dil):
    if dil == 1:
        return src_ref[...].astype(F32)
    slabs, rows, _ = stage_ref.shape
    w = slabs * LANES
    for r in range(dil):
        for j in range(slabs):
            piece = src_ref[:, r * w + j * LANES:r * w + (j + 1) * LANES]
            stage_ref[j, pl.ds(r, rows // dil, stride=dil), :] = piece.astype(F32)
    parts = [stage_ref[j] for j in range(slabs)]
    return parts[0] if slabs == 1 else jnp.concatenate(parts, axis=1)


def _post_kernel(*refs, dils):
    n_groups = max(len(dils), 1)
    it = iter(refs)
    x_ref = next(it)
    o_refs = [next(it) for _ in range(n_groups)]
    lse_refs = [next(it) for _ in dils]
    p_ref = next(it)
    wo_ref, nf_ref, wg_ref, wu_ref, wd_ref, np_ref, pg_ref, pp_ref = (next(it) for _ in range(8))
    out_ref = next(it)
    o_stage = {g: next(it) for g, d in enumerate(dils) if d > 1}
    l_stage = {g: next(it) for g, d in enumerate(dils) if d > 1}

    if not dils:
        o = o_refs[0][...]
    else:
        os_ = [_gather_residues(o_refs[g], o_stage.get(g), d) for g, d in enumerate(dils)]
        lses = [_gather_residues(lse_refs[g], l_stage.get(g), d) for g, d in enumerate(dils)]
        mx = functools.reduce(jnp.maximum, lses)
        es = [jnp.exp(l - mx) for l in lses]
        inv = 1.0 / functools.reduce(lambda a, b: a + b, es)
        tm = lses[0].shape[0]
        lane = lax.broadcasted_iota(jnp.int32, (tm, LANES), 1)
        pieces = []
        for g in range(n_groups):
            alpha = es[g] * inv
            for a in range(B_GRP):
                slab = os_[g][:, a * LANES:(a + 1) * LANES]
                mult = jnp.where(lane < HEAD_DIM, alpha[:, a:a + 1], alpha[:, B_GRP + a:B_GRP + a + 1])
                pieces.append((slab * mult).astype(BF16))
        o = jnp.concatenate(pieces, axis=1)

    x1 = x_ref[...] + jnp.dot(o, wo_ref[...], preferred_element_type=F32)
    h = _rmsnorm_rows(x1, nf_ref[...]).astype(BF16)
    acc = None
    for c, w in _ffn_chunks():
        g = jnp.dot(h, wg_ref[:, c:c + w], preferred_element_type=F32)
        u = jnp.dot(h, wu_ref[:, c:c + w], preferred_element_type=F32)
        act = (g * jax.nn.sigmoid(g) * u).astype(BF16)
        d = jnp.dot(act, wd_ref[c:c + w, :], preferred_element_type=F32)
        acc = d if acc is None else acc + d
    x2 = x1 + acc
    hp = _rmsnorm_rows(x2, np_ref[...]).astype(BF16)
    gate = jax.nn.sigmoid(jnp.dot(hp, pg_ref[...], preferred_element_type=F32))
    proj = jnp.dot(p_ref[...].astype(BF16), pp_ref[...], preferred_element_type=F32)
    out_ref[...] = x2 + gate * proj


def _post_call(x2d, os_, lses, dils, p3d, layer, wo, nf, wg, wu, wd, npl, pg, pp, *, tm):
    t = x2d.shape[0]
    row = lambda r: (r, 0)
    group_dils = dils if dils else (1,)
    in_specs = [pl.BlockSpec((tm, D_MODEL), row)]
    in_specs += [pl.BlockSpec((tm // d, o.shape[1]), row) for o, d in zip(os_, group_dils)]
    in_specs += [pl.BlockSpec((tm // d, l.shape[1]), row) for l, d in zip(lses, dils)]
    in_specs.append(pl.BlockSpec((None, tm, PLE_DIM), lambda r: (layer, r, 0)))
    weights = [wo, nf, wg, wu, wd, npl, pg, pp]
    in_specs += [_const_spec(w.shape) for w in weights]
    scratch = [pltpu.VMEM((B_GRP, tm, LANES), F32) for d in dils if d > 1]
    scratch += [pltpu.VMEM((1, tm, LANES), F32) for d in dils if d > 1]
    return pl.pallas_call(
        functools.partial(_post_kernel, dils=tuple(dils)),
        grid=(t // tm,),
        in_specs=in_specs,
        out_specs=pl.BlockSpec((tm, D_MODEL), row),
        out_shape=jax.ShapeDtypeStruct((t, D_MODEL), F32),
        scratch_shapes=scratch,
        compiler_params=pltpu.CompilerParams(
            dimension_semantics=("arbitrary",), vmem_limit_bytes=VMEM_LIMIT),
        name="post",
    )(x2d, *os_, *lses, p3d, *weights)


def _alibi_slopes(n):
    return 2.0 ** (-8.0 * np.arange(1, n + 1, dtype=np.float64) / n)


def _a_head_order():
    return [(2 * s + par) * A_GRP + a
            for s in range(A_KV_HEADS // 2) for a in range(A_GRP) for par in range(2)]


def _b_head_order():
    return [g * B_Q_PER_GROUP + par * B_GRP + a
            for g in range(len(B_GROUPS)) for a in range(B_GRP) for par in range(2)]


def _slab_param_order(n_slabs, grp):
    return [(s * grp + a) * 2 + par for s in range(n_slabs) for par in range(2) for a in range(grp)]


def _block_diag_ones():
    r = np.arange(MXU_DIM) // HEAD_DIM
    return jnp.asarray((r[:, None] == r[None, :]).astype(np.float32), dtype=BF16)


def _prep_qkv_weights(wqkv, wo, q_gain, k_gain, hq, hkv, order):
    cq, ck = hq * HEAD_DIM, hkv * HEAD_DIM
    order = np.asarray(order)
    wq = wqkv[:, :cq].reshape(D_MODEL, hq, HEAD_DIM)[:, order].reshape(D_MODEL, cq)
    w = jnp.concatenate([wq, wqkv[:, cq:]], axis=1).astype(BF16)
    wo_p = wo.reshape(hq, HEAD_DIM, D_MODEL)[order].reshape(cq, D_MODEL).astype(BF16)
    gq = jnp.tile(q_gain * (HEAD_DIM ** -0.5 * LOG2E), hq).reshape(1, cq)
    gk = jnp.tile(k_gain, hkv).reshape(1, ck)
    return w, wo_p, gq, gk


def _trunk(x, p, norm_mix, a_prep, b_prep, ffn_prep, *, tm):
    b, s, _ = x.shape
    t = b * s
    x2d = x.reshape(t, D_MODEL)
    p3d = p.reshape(p.shape[0], t, PLE_DIM)
    bd = _block_diag_ones()

    w, wo_p, gq, gk, slopes, sink = a_prep
    cq, ck = A_Q_HEADS * HEAD_DIM, A_KV_HEADS * HEAD_DIM
    q, k, v = _qkv_call(x2d, norm_mix[0].reshape(1, -1), w, gq, gk, bd, cq=cq, ck=ck,
                        outs=[(0, cq, 1), (cq, ck, 1), (cq + ck, ck, 1)], tm=tm)
    o = _attn_call(q.reshape(b, s, cq), k.reshape(b, s, ck), v.reshape(b, s, ck), slopes, sink,
                   radius=A_RADIUS, stride=1, grp=A_GRP, want_lse=False)[0]
    x2d = _post_call(x2d, [o.reshape(t, cq)], [], (), p3d, 0, wo_p, *ffn_prep[0], tm=tm)

    w, wo_p, gq, gk, slopes = b_prep
    cq, ck = B_Q_HEADS * HEAD_DIM, B_KV_HEADS * HEAD_DIM
    gq_w = B_GRP * LANES
    dils = tuple(d for _, d in B_GROUPS)
    outs = []
    for g, d in enumerate(dils):
        outs += [(g * gq_w, gq_w, d), (cq + g * LANES, LANES, d), (cq + ck + g * LANES, LANES, d)]
    qkv = _qkv_call(x2d, norm_mix[1].reshape(1, -1), w, gq, gk, bd, cq=cq, ck=ck, outs=outs, tm=tm)
    os_, lses = [], []
    for g, (window, d) in enumerate(B_GROUPS):
        radius = window // (2 * d)
        qg, kg, vg = (a.reshape(b, s // d, a.shape[1]) for a in qkv[3 * g:3 * g + 3])
        og, lg = _attn_call(qg, kg, vg, slopes[g:g + 1], None,
                            radius=radius, stride=d, grp=B_GRP, want_lse=True)
        os_.append(og.reshape(t // d, d * gq_w))
        lses.append(lg.reshape(t // d, d * LANES))
    x2d = _post_call(x2d, os_, lses, dils, p3d, 1, wo_p, *ffn_prep[1], tm=tm)
    return x2d.reshape(b, s, D_MODEL)


def kernel(x_prompt, x_sample, p_prompt, p_sample, norm_mix, norm_ffn, norm_ple,
           a_wqkv, a_wo, a_q_gain, a_k_gain, a_sink,
           b_wqkv, b_wo, b_q_gain, b_k_gain,
           ffn_w_gate, ffn_w_up, ffn_w_down, ple_w_gate, ple_w_proj):
    a_order, b_order = _a_head_order(), _b_head_order()
    a_slabs = A_KV_HEADS // 2

    a_par = np.asarray(a_order)[_slab_param_order(a_slabs, A_GRP)]
    a_slopes = jnp.asarray(_alibi_slopes(A_Q_HEADS)[a_par].reshape(a_slabs, 2 * A_GRP), dtype=F32)
    a_prep = _prep_qkv_weights(a_wqkv[0], a_wo[0], a_q_gain[0], a_k_gain[0],
                               A_Q_HEADS, A_KV_HEADS, a_order) + (a_slopes, a_sink[0][a_par] * LOG2E)

    b_par = np.asarray(b_order)[_slab_param_order(len(B_GROUPS), B_GRP)]
    b_slopes = jnp.asarray(_alibi_slopes(B_Q_HEADS)[b_par].reshape(len(B_GROUPS), 2 * B_GRP), dtype=F32)
    b_prep = _prep_qkv_weights(b_wqkv[0], b_wo[0], b_q_gain[0], b_k_gain[0],
                               B_Q_HEADS, B_KV_HEADS, b_order) + (b_slopes,)

    ffn_prep = []
    for i in range(norm_ffn.shape[0]):
        ffn_prep.append((norm_ffn[i].reshape(1, -1), ffn_w_gate[i].astype(BF16), ffn_w_up[i].astype(BF16),
                         ffn_w_down[i].astype(BF16), norm_ple[i].reshape(1, -1),
                         ple_w_gate[i].astype(BF16), ple_w_proj[i].astype(BF16)))

    run = functools.partial(_trunk, norm_mix=norm_mix, a_prep=a_prep, b_prep=b_prep,
                            ffn_prep=ffn_prep, tm=512)
    return (run(x_prompt, p_prompt), run(x_sample, p_sample))
```

```python
import functools
import math

import numpy as np
import jax
import jax.numpy as jnp
from jax import lax
from jax.experimental import pallas as pl
from jax.experimental.pallas import tpu as pltpu

D_MODEL = 1024
HEAD_DIM = 64
PLE_DIM = 256
FFN_HIDDEN = 2816
EPS = 1e-6

A_Q_HEADS = 16
A_KV_HEADS = 4
A_RADIUS = 128
B_GROUPS = ((128, 1), (512, 4), (2048, 16))
B_Q_PER_GROUP = 6
B_KV_PER_GROUP = 2
B_Q_HEADS = B_Q_PER_GROUP * len(B_GROUPS)
B_KV_HEADS = B_KV_PER_GROUP * len(B_GROUPS)
A_GRP = A_Q_HEADS // A_KV_HEADS
B_GRP = B_Q_PER_GROUP // B_KV_PER_GROUP

LANES = 128
MXU_DIM = 256
SUB_ROWS = 128
MASK_BIAS = -1e32
VMEM_LIMIT = 56 * 1024 * 1024
LOG2E = math.log2(math.e)
LN2 = math.log(2.0)
FAST_SOFTMAX_MAX_BOUND = 30.0 * LOG2E

F32 = jnp.float32
BF16 = jnp.bfloat16


def _const_spec(shape):
    nd = len(shape)
    return pl.BlockSpec(shape, lambda *_: (0,) * nd, pipeline_mode=pl.Buffered(1))


def _rmsnorm_rows(x, gain):
    ms = jnp.mean(x * x, axis=-1, keepdims=True)
    return x * lax.rsqrt(ms + EPS) * gain


def _head_sumsq(t, bd):
    width = t.shape[1]
    sq = (t * t).astype(BF16)
    parts = []
    c = 0
    while c < width:
        w = min(MXU_DIM, width - c)
        parts.append(jnp.dot(sq[:, c:c + w], bd[:w, :w], preferred_element_type=F32))
        c += w
    return parts[0] if len(parts) == 1 else jnp.concatenate(parts, axis=1)


def _qkv_kernel(x_ref, gn_ref, w_ref, gq_ref, gk_ref, bd_ref, *rest, cq, ck, outs):
    out_refs = rest[:len(outs)]
    stage_ref = rest[len(outs)] if len(rest) > len(outs) else None
    tm = x_ref.shape[0]
    h = _rmsnorm_rows(x_ref[...], gn_ref[...]).astype(BF16)
    qkv = jnp.dot(h, w_ref[...], preferred_element_type=F32)
    bd = bd_ref[...]
    q = qkv[:, :cq]
    k = qkv[:, cq:cq + ck]
    inv_hd = 1.0 / HEAD_DIM
    q = q * lax.rsqrt(_head_sumsq(q, bd) * inv_hd + EPS) * gq_ref[...]
    k = k * lax.rsqrt(_head_sumsq(k, bd) * inv_hd + EPS) * gk_ref[...]
    full = jnp.concatenate([q, k, qkv[:, cq + ck:]], axis=1)
    if stage_ref is not None:
        for j in range(stage_ref.shape[0]):
            stage_ref[j] = full[:, j * LANES:(j + 1) * LANES]
    for (off, w, dil), o_ref in zip(outs, out_refs):
        if dil == 1:
            o_ref[...] = full[:, off:off + w].astype(BF16)
        else:
            rows = tm // dil
            for r in range(dil):
                for j in range(w // LANES):
                    piece = stage_ref[off // LANES + j, pl.ds(r, rows, stride=dil), :]
                    o_ref[:, r * w + j * LANES:r * w + (j + 1) * LANES] = piece.astype(BF16)


def _qkv_call(x2d, gn, w, gq, gk, bd, *, cq, ck, outs, tm):
    t = x2d.shape[0]
    c = cq + 2 * ck
    row = lambda r: (r, 0)
    staged = any(d > 1 for _, _, d in outs)
    return pl.pallas_call(
        functools.partial(_qkv_kernel, cq=cq, ck=ck, outs=tuple(outs)),
        grid=(t // tm,),
        in_specs=[
            pl.BlockSpec((tm, D_MODEL), row),
            _const_spec((1, D_MODEL)),
            _const_spec((D_MODEL, c)),
            _const_spec((1, cq)),
            _const_spec((1, ck)),
            _const_spec((MXU_DIM, MXU_DIM)),
        ],
        out_specs=[pl.BlockSpec((tm // d, d * w_), row) for _, w_, d in outs],
        out_shape=[jax.ShapeDtypeStruct((t // d, d * w_), BF16) for _, w_, d in outs],
        scratch_shapes=[pltpu.VMEM((c // LANES, tm, LANES), F32)] if staged else [],
        compiler_params=pltpu.CompilerParams(
            dimension_semantics=("arbitrary",), vmem_limit_bytes=VMEM_LIMIT),
        name="qkv",
    )(x2d, gn, w, gq, gk, bd)


def _attn_kernel(*refs, radius, bq, grp, has_sink, has_lse, heads_by_col):
    nsub = bq // SUB_ROWS
    wp = SUB_ROWS + 2 * radius
    g_rows = grp * SUB_ROWS
    it = iter(refs)
    fast_ref = next(it)
    hp_ref = next(it)
    q_ref = next(it)
    kp_ref, kc_ref, kn_ref, vp_ref, vc_ref, vn_ref = (next(it) for _ in range(6))
    tab_refs = [next(it) for _ in range(nsub)]
    o_ref = next(it)
    lse_ref = next(it) if has_lse else None

    head0 = pl.program_id(0) * (2 * grp) if heads_by_col else 0

    def body(fast):
        kwin = jnp.concatenate([kp_ref[0], kc_ref[0], kn_ref[0]], axis=0)
        vwin = jnp.concatenate([vp_ref[0], vc_ref[0], vn_ref[0]], axis=0)
        klane = lax.broadcasted_iota(jnp.int32, kwin.shape, 1)
        kzero = jnp.zeros_like(kwin)
        k_half = [jnp.where(klane < HEAD_DIM, kwin, kzero),
                  jnp.where(klane >= HEAD_DIM, kwin, kzero)]
        olane = lax.broadcasted_iota(jnp.int32, (SUB_ROWS, LANES), 1)
        ones = jnp.ones((wp, LANES), BF16)

        for t in range(nsub):
            r0 = t * SUB_ROWS
            qst = jnp.concatenate(
                [q_ref[0, r0:r0 + SUB_ROWS, a * LANES:(a + 1) * LANES] for a in range(grp)], axis=0)
            probs, dens, maxes = [], [], []
            for par in range(2):
                sc = lax.dot_general(qst, k_half[par][r0:r0 + wp], (((1,), (1,)), ((), ())),
                                     preferred_element_type=F32)
                if fast:
                    probs.append(jnp.exp2(sc + tab_refs[t][par * g_rows:(par + 1) * g_rows, :]).astype(BF16))
                    continue
                for a in range(grp):
                    h = par * grp + a
                    sa = sc[a * SUB_ROWS:(a + 1) * SUB_ROWS] + tab_refs[t][h * SUB_ROWS:(h + 1) * SUB_ROWS, :]
                    m = jnp.max(sa, axis=-1, keepdims=True)
                    if has_sink:
                        sink = hp_ref[1, head0 + h]
                        m = jnp.maximum(m, sink)
                    p = jnp.exp2(sa - m)
                    den = jnp.sum(p, axis=-1, keepdims=True)
                    if has_sink:
                        den = den + jnp.exp2(sink - m)
                    probs.append(p.astype(BF16))
                    dens.append(den)
                    maxes.append(m)
            vsub = vwin[r0:r0 + wp]
            if fast:
                vsub = jnp.concatenate([vsub, ones], axis=1)
            pv = jnp.dot(jnp.concatenate(probs, axis=0), vsub, preferred_element_type=F32)
            lse_acc = jnp.zeros((SUB_ROWS, LANES), F32) if has_lse else None
            for a in range(grp):
                halves = []
                for par in range(2):
                    h = par * grp + a
                    rows = slice(h * SUB_ROWS, (h + 1) * SUB_ROWS)
                    if fast:
                        den = pv[rows, LANES:]
                        if has_sink:
                            den = den + hp_ref[2, head0 + h]
                        num = pv[rows, :LANES]
                        shift = hp_ref[0, head0 + h]
                    else:
                        den, num, shift = dens[h], pv[rows], maxes[h]
                    halves.append(num / den)
                    if has_lse:
                        lse_acc = jnp.where(olane == h, LN2 * shift + jnp.log(den), lse_acc)
                o_ref[0, r0:r0 + SUB_ROWS, a * LANES:(a + 1) * LANES] = jnp.where(
                    olane < HEAD_DIM, halves[0], halves[1]).astype(BF16)
            if has_lse:
                lse_ref[0, r0:r0 + SUB_ROWS, :] = lse_acc

    @pl.when(fast_ref[0] != 0)
    def _():
        body(True)

    @pl.when(fast_ref[0] == 0)
    def _():
        body(False)


def _bias_tables(radius, stride, slopes, stab, grp):
    wp = SUB_ROWS + 2 * radius
    i = np.arange(SUB_ROWS)[:, None]
    j = np.arange(wp)[None, :]
    rel = j - radius - i
    band = np.abs(rel) <= radius
    tabs = []
    for code in range(4):
        valid = band
        if code & 1:
            valid = valid & (j >= radius)
        if code & 2:
            valid = valid & (j < SUB_ROWS + radius)
        tabs.append(np.where(valid, -float(stride) * LOG2E * np.abs(rel), MASK_BIAS))
    base = jnp.asarray(np.stack(tabs), dtype=F32)
    slabs = slopes.shape[0]
    full = base[:, None, None] * slopes[None, :, :, None, None] - stab[None, :, :, None, None]
    return full.reshape(4, slabs, 2 * grp * SUB_ROWS, wp)


def _softmax_params(q_gain_scaled, k_gain, sink2, n_heads):
    bound = HEAD_DIM * jnp.max(jnp.abs(q_gain_scaled)) * jnp.max(jnp.abs(k_gain))
    fast = bound <= FAST_SOFTMAX_MAX_BOUND
    sink2 = jnp.zeros((n_heads,), F32) if sink2 is None else sink2
    stab = jnp.where(fast, jnp.maximum(bound, sink2), 0.0)
    hp = jnp.stack([stab, sink2 - stab, jnp.exp2(sink2 - stab)])
    return fast.astype(jnp.int32).reshape(1), hp


def _attn_call(q, k, v, slopes, fast, hp, *, radius, stride, grp, has_sink, want_lse):
    n, length, _ = q.shape
    cols = k.shape[2] // LANES
    heads_by_col = has_sink
    bq = next(c for c in (512, 256, SUB_ROWS) if length % c == 0)
    assert bq % radius == 0
    nsub = bq // SUB_ROWS
    wp = SUB_ROWS + 2 * radius
    g_rows = 2 * grp * SUB_ROWS
    nblk = length // bq
    halo_per_blk = bq // radius
    n_halo = length // radius
    total_sub = length // SUB_ROWS
    tables = _bias_tables(radius, stride, slopes, hp[0].reshape(slopes.shape), grp)

    def prev_map(c, b, i):
        return (b, jnp.maximum(i * halo_per_blk - 1, 0), c)

    def next_map(c, b, i):
        return (b, jnp.minimum((i + 1) * halo_per_blk, n_halo - 1), c)

    cur_map = lambda c, b, i: (b, i, c)

    def tab_map(t):
        def f(c, b, i):
            u = i * nsub + t
            code = (u == 0).astype(jnp.int32) + 2 * (u == total_sub - 1).astype(jnp.int32)
            return (code, c if heads_by_col else 0, 0, 0)
        return f

    smem = pl.BlockSpec(memory_space=pltpu.SMEM)
    in_specs = [smem, smem, pl.BlockSpec((1, bq, grp * LANES), cur_map)]
    args = [fast, hp, q]
    for arr in (k, v):
        in_specs += [pl.BlockSpec((1, radius, LANES), prev_map),
                     pl.BlockSpec((1, bq, LANES), cur_map),
                     pl.BlockSpec((1, radius, LANES), next_map)]
        args += [arr, arr, arr]
    for t in range(nsub):
        in_specs.append(pl.BlockSpec((None, None, g_rows, wp), tab_map(t)))
        args.append(tables)

    out_specs = [pl.BlockSpec((1, bq, grp * LANES), cur_map)]
    out_shape = [jax.ShapeDtypeStruct(q.shape, BF16)]
    if want_lse:
        out_specs.append(pl.BlockSpec((1, bq, LANES), cur_map))
        out_shape.append(jax.ShapeDtypeStruct((n, length, cols * LANES), F32))

    return pl.pallas_call(
        functools.partial(_attn_kernel, radius=radius, bq=bq, grp=grp, has_sink=has_sink,
                          has_lse=want_lse, heads_by_col=heads_by_col),
        grid=(cols, n, nblk),
        in_specs=in_specs,
        out_specs=out_specs,
        out_shape=out_shape,
        compiler_params=pltpu.CompilerParams(
            dimension_semantics=("arbitrary", "arbitrary", "arbitrary"),
            vmem_limit_bytes=VMEM_LIMIT),
        name="attn",
    )(*args)


def _ffn_chunks():
    chunks, c = [], 0
    while c < FFN_HIDDEN:
        w = min(2 * MXU_DIM, FFN_HIDDEN - c)
        chunks.append((c, w))
        c += w
    return chunks


def _gather_residues(src_ref, stage_ref, dil):
    if dil == 1:
        return src_ref[...].astype(F32)
    slabs, rows, _ = stage_ref.shape
    w = slabs * LANES
    for r in range(dil):
        for j in range(slabs):
            piece = src_ref[:, r * w + j * LANES:r * w + (j + 1) * LANES]
            stage_ref[j, pl.ds(r, rows // dil, stride=dil), :] = piece.astype(F32)
    parts = [stage_ref[j] for j in range(slabs)]
    return parts[0] if slabs == 1 else jnp.concatenate(parts, axis=1)


def _post_kernel(*refs, dils):
    n_groups = max(len(dils), 1)
    it = iter(refs)
    x_ref = next(it)
    o_refs = [next(it) for _ in range(n_groups)]
    lse_refs = [next(it) for _ in dils]
    expand_ref = next(it) if dils else None
    p_ref = next(it)
    wo_ref, nf_ref, wg_ref, wu_ref, wd_ref, np_ref, pg_ref, pp_ref = (next(it) for _ in range(8))
    out_ref = next(it)
    o_stage = {g: next(it) for g, d in enumerate(dils) if d > 1}
    l_stage = {g: next(it) for g, d in enumerate(dils) if d > 1}

    if not dils:
        o = o_refs[0][...]
    else:
        os_ = [_gather_residues(o_refs[g], o_stage.get(g), d) for g, d in enumerate(dils)]
        lses = [_gather_residues(lse_refs[g], l_stage.get(g), d) for g, d in enumerate(dils)]
        mx = functools.reduce(jnp.maximum, lses)
        es = [jnp.exp(l - mx) for l in lses]
        inv = 1.0 / functools.reduce(lambda a, b: a + b, es)
        expand = expand_ref[...]
        pieces = []
        for g in range(n_groups):
            alpha = es[g] * inv
            hi = alpha.astype(BF16)
            lo = (alpha - hi.astype(F32)).astype(BF16)
            spread = (jnp.dot(hi, expand, preferred_element_type=F32)
                      + jnp.dot(lo, expand, preferred_element_type=F32))
            pieces.append((os_[g] * spread).astype(BF16))
        o = jnp.concatenate(pieces, axis=1)

    x1 = x_ref[...] + jnp.dot(o, wo_ref[...], preferred_element_type=F32)
    h = _rmsnorm_rows(x1, nf_ref[...]).astype(BF16)
    acc = None
    for c, w in _ffn_chunks():
        g = jnp.dot(h, wg_ref[:, c:c + w], preferred_element_type=F32)
        u = jnp.dot(h, wu_ref[:, c:c + w], preferred_element_type=F32)
        act = (g * jax.nn.sigmoid(g) * u).astype(BF16)
        d = jnp.dot(act, wd_ref[c:c + w, :], preferred_element_type=F32)
        acc = d if acc is None else acc + d
    x2 = x1 + acc
    hp = _rmsnorm_rows(x2, np_ref[...]).astype(BF16)
    gate = jax.nn.sigmoid(jnp.dot(hp, pg_ref[...], preferred_element_type=F32))
    proj = jnp.dot(p_ref[...].astype(BF16), pp_ref[...], preferred_element_type=F32)
    out_ref[...] = x2 + gate * proj


def _head_expand_matrix():
    c = np.arange(B_GRP * LANES)
    head = ((c % LANES) // HEAD_DIM) * B_GRP + c // LANES
    return jnp.asarray((np.arange(LANES)[:, None] == head[None, :]).astype(np.float32), dtype=BF16)


def _post_call(x2d, os_, lses, dils, p3d, layer, wo, nf, wg, wu, wd, npl, pg, pp, *, tm):
    t = x2d.shape[0]
    row = lambda r: (r, 0)
    group_dils = dils if dils else (1,)
    in_specs = [pl.BlockSpec((tm, D_MODEL), row)]
    in_specs += [pl.BlockSpec((tm // d, o.shape[1]), row) for o, d in zip(os_, group_dils)]
    in_specs += [pl.BlockSpec((tm // d, l.shape[1]), row) for l, d in zip(lses, dils)]
    extra = []
    if dils:
        extra.append(_head_expand_matrix())
        in_specs.append(_const_spec(extra[0].shape))
    in_specs.append(pl.BlockSpec((None, tm, PLE_DIM), lambda r: (layer, r, 0)))
    weights = [wo, nf, wg, wu, wd, npl, pg, pp]
    in_specs += [_const_spec(w.shape) for w in weights]
    scratch = [pltpu.VMEM((B_GRP, tm, LANES), F32) for d in dils if d > 1]
    scratch += [pltpu.VMEM((1, tm, LANES), F32) for d in dils if d > 1]
    return pl.pallas_call(
        functools.partial(_post_kernel, dils=tuple(dils)),
        grid=(t // tm,),
        in_specs=in_specs,
        out_specs=pl.BlockSpec((tm, D_MODEL), row),
        out_shape=jax.ShapeDtypeStruct((t, D_MODEL), F32),
        scratch_shapes=scratch,
        compiler_params=pltpu.CompilerParams(
            dimension_semantics=("arbitrary",), vmem_limit_bytes=VMEM_LIMIT),
        name="post",
    )(x2d, *os_, *lses, *extra, p3d, *weights)


def _alibi_slopes(n):
    return 2.0 ** (-8.0 * np.arange(1, n + 1, dtype=np.float64) / n)


def _a_head_order():
    return [(2 * s + par) * A_GRP + a
            for s in range(A_KV_HEADS // 2) for a in range(A_GRP) for par in range(2)]


def _b_head_order():
    return [g * B_Q_PER_GROUP + par * B_GRP + a
            for g in range(len(B_GROUPS)) for a in range(B_GRP) for par in range(2)]


def _slab_param_order(n_slabs, grp):
    return [(s * grp + a) * 2 + par for s in range(n_slabs) for par in range(2) for a in range(grp)]


def _block_diag_ones():
    r = np.arange(MXU_DIM) // HEAD_DIM
    return jnp.asarray((r[:, None] == r[None, :]).astype(np.float32), dtype=BF16)


def _prep_qkv_weights(wqkv, wo, q_gain, k_gain, hq, hkv, order):
    cq, ck = hq * HEAD_DIM, hkv * HEAD_DIM
    order = np.asarray(order)
    wq = wqkv[:, :cq].reshape(D_MODEL, hq, HEAD_DIM)[:, order].reshape(D_MODEL, cq)
    w = jnp.concatenate([wq, wqkv[:, cq:]], axis=1).astype(BF16)
    wo_p = wo.reshape(hq, HEAD_DIM, D_MODEL)[order].reshape(cq, D_MODEL).astype(BF16)
    gq = jnp.tile(q_gain * (HEAD_DIM ** -0.5 * LOG2E), hq).reshape(1, cq)
    gk = jnp.tile(k_gain, hkv).reshape(1, ck)
    return w, wo_p, gq, gk


def _trunk(x, p, norm_mix, a_prep, b_prep, ffn_prep, *, tm):
    b, s, _ = x.shape
    t = b * s
    x2d = x.reshape(t, D_MODEL)
    p3d = p.reshape(p.shape[0], t, PLE_DIM)
    bd = _block_diag_ones()

    w, wo_p, gq, gk, slopes, fast, hp = a_prep
    cq, ck = A_Q_HEADS * HEAD_DIM, A_KV_HEADS * HEAD_DIM
    q, k, v = _qkv_call(x2d, norm_mix[0].reshape(1, -1), w, gq, gk, bd, cq=cq, ck=ck,
                        outs=[(0, cq, 1), (cq, ck, 1), (cq + ck, ck, 1)], tm=tm)
    o = _attn_call(q.reshape(b, s, cq), k.reshape(b, s, ck), v.reshape(b, s, ck), slopes, fast, hp,
                   radius=A_RADIUS, stride=1, grp=A_GRP, has_sink=True, want_lse=False)[0]
    x2d = _post_call(x2d, [o.reshape(t, cq)], [], (), p3d, 0, wo_p, *ffn_prep[0], tm=tm)

    w, wo_p, gq, gk, slopes, fast, hp = b_prep
    cq, ck = B_Q_HEADS * HEAD_DIM, B_KV_HEADS * HEAD_DIM
    gq_w = B_GRP * LANES
    dils = tuple(d for _, d in B_GROUPS)
    outs = []
    for g, d in enumerate(dils):
        outs += [(g * gq_w, gq_w, d), (cq + g * LANES, LANES, d), (cq + ck + g * LANES, LANES, d)]
    qkv = _qkv_call(x2d, norm_mix[1].reshape(1, -1), w, gq, gk, bd, cq=cq, ck=ck, outs=outs, tm=tm)
    os_, lses = [], []
    for g, (window, d) in enumerate(B_GROUPS):
        radius = window // (2 * d)
        qg, kg, vg = (a.reshape(b, s // d, a.shape[1]) for a in qkv[3 * g:3 * g + 3])
        og, lg = _attn_call(qg, kg, vg, slopes[g:g + 1], fast, hp[:, g * B_Q_PER_GROUP:(g + 1) * B_Q_PER_GROUP],
                            radius=radius, stride=d, grp=B_GRP, has_sink=False, want_lse=True)
        os_.append(og.reshape(t // d, d * gq_w))
        lses.append(lg.reshape(t // d, d * LANES))
    x2d = _post_call(x2d, os_, lses, dils, p3d, 1, wo_p, *ffn_prep[1], tm=tm)
    return x2d.reshape(b, s, D_MODEL)


def kernel(x_prompt, x_sample, p_prompt, p_sample, norm_mix, norm_ffn, norm_ple,
           a_wqkv, a_wo, a_q_gain, a_k_gain, a_sink,
           b_wqkv, b_wo, b_q_gain, b_k_gain,
           ffn_w_gate, ffn_w_up, ffn_w_down, ple_w_gate, ple_w_proj):
    a_order, b_order = _a_head_order(), _b_head_order()
    a_slabs = A_KV_HEADS // 2

    a_par = np.asarray(a_order)[_slab_param_order(a_slabs, A_GRP)]
    a_slopes = jnp.asarray(_alibi_slopes(A_Q_HEADS)[a_par].reshape(a_slabs, 2 * A_GRP), dtype=F32)
    a_prep = _prep_qkv_weights(a_wqkv[0], a_wo[0], a_q_gain[0], a_k_gain[0], A_Q_HEADS, A_KV_HEADS, a_order)
    a_prep += (a_slopes,) + _softmax_params(a_prep[2][0, :HEAD_DIM], a_k_gain[0], a_sink[0][a_par] * LOG2E,
                                            A_Q_HEADS)

    b_par = np.asarray(b_order)[_slab_param_order(len(B_GROUPS), B_GRP)]
    b_slopes = jnp.asarray(_alibi_slopes(B_Q_HEADS)[b_par].reshape(len(B_GROUPS), 2 * B_GRP), dtype=F32)
    b_prep = _prep_qkv_weights(b_wqkv[0], b_wo[0], b_q_gain[0], b_k_gain[0], B_Q_HEADS, B_KV_HEADS, b_order)
    b_prep += (b_slopes,) + _softmax_params(b_prep[2][0, :HEAD_DIM], b_k_gain[0], None, B_Q_HEADS)

    ffn_prep = []
    for i in range(norm_ffn.shape[0]):
        ffn_prep.append((norm_ffn[i].reshape(1, -1), ffn_w_gate[i].astype(BF16), ffn_w_up[i].astype(BF16),
                         ffn_w_down[i].astype(BF16), norm_ple[i].reshape(1, -1),
                         ple_w_gate[i].astype(BF16), ple_w_proj[i].astype(BF16)))

    run = functools.partial(_trunk, norm_mix=norm_mix, a_prep=a_prep, b_prep=b_prep,
                            ffn_prep=ffn_prep, tm=512)
    return (run(x_prompt, p_prompt), run(x_sample, p_sample))
```

```python
import functools
import math

import numpy as np
import jax
import jax.numpy as jnp
from jax import lax
from jax.experimental import pallas as pl
from jax.experimental.pallas import tpu as pltpu

D_MODEL = 1024
HEAD_DIM = 64
PLE_DIM = 256
FFN_HIDDEN = 2816
EPS = 1e-6

A_Q_HEADS = 16
A_KV_HEADS = 4
A_RADIUS = 128
B_GROUPS = ((128, 1), (512, 4), (2048, 16))
B_Q_PER_GROUP = 6
B_KV_PER_GROUP = 2
B_Q_HEADS = B_Q_PER_GROUP * len(B_GROUPS)
B_KV_HEADS = B_KV_PER_GROUP * len(B_GROUPS)
A_GRP = A_Q_HEADS // A_KV_HEADS
B_GRP = B_Q_PER_GROUP // B_KV_PER_GROUP

LANES = 128
MXU_DIM = 256
SUB_ROWS = 128
MASK_BIAS = -1e32
VMEM_LIMIT = 56 * 1024 * 1024
LOG2E = math.log2(math.e)
LN2 = math.log(2.0)
FAST_SOFTMAX_MAX_BOUND = 30.0 * LOG2E

F32 = jnp.float32
BF16 = jnp.bfloat16


def _const_spec(shape):
    nd = len(shape)
    return pl.BlockSpec(shape, lambda *_: (0,) * nd, pipeline_mode=pl.Buffered(1))


def _rmsnorm_rows(x, gain):
    ms = jnp.mean(x * x, axis=-1, keepdims=True)
    return x * lax.rsqrt(ms + EPS) * gain


def _head_sumsq(t):
    width = t.shape[1]
    lane = lax.broadcasted_iota(jnp.int32, (t.shape[0], LANES), 1)
    parts = []
    for c in range(0, width, LANES):
        blk = t[:, c:c + LANES]
        sq = blk * blk
        tot = jnp.sum(sq, axis=-1, keepdims=True)
        lo = jnp.sum(jnp.where(lane < HEAD_DIM, sq, 0.0), axis=-1, keepdims=True)
        parts.append(jnp.where(lane < HEAD_DIM, lo, tot - lo))
    return parts[0] if len(parts) == 1 else jnp.concatenate(parts, axis=1)


def _qkv_kernel(x_ref, gn_ref, w_ref, gq_ref, gk_ref, *rest, cq, ck, outs):
    out_refs = rest[:len(outs)]
    stage_ref = rest[len(outs)] if len(rest) > len(outs) else None
    tm = x_ref.shape[0]
    h = _rmsnorm_rows(x_ref[...], gn_ref[...]).astype(BF16)
    qkv = jnp.dot(h, w_ref[...], preferred_element_type=F32)
    q = qkv[:, :cq]
    k = qkv[:, cq:cq + ck]
    q = q * lax.rsqrt(_head_sumsq(q) + HEAD_DIM * EPS) * gq_ref[...]
    k = k * lax.rsqrt(_head_sumsq(k) + HEAD_DIM * EPS) * gk_ref[...]
    full = jnp.concatenate([q, k, qkv[:, cq + ck:]], axis=1)
    if stage_ref is not None:
        for j in range(stage_ref.shape[0]):
            stage_ref[j] = full[:, j * LANES:(j + 1) * LANES]
    for (off, w, dil), o_ref in zip(outs, out_refs):
        if dil == 1:
            o_ref[...] = full[:, off:off + w].astype(BF16)
        else:
            rows = tm // dil
            for r in range(dil):
                for j in range(w // LANES):
                    piece = stage_ref[off // LANES + j, pl.ds(r, rows, stride=dil), :]
                    o_ref[:, r * w + j * LANES:r * w + (j + 1) * LANES] = piece.astype(BF16)


def _qkv_call(x2d, gn, w, gq, gk, *, cq, ck, outs, tm):
    t = x2d.shape[0]
    c = cq + 2 * ck
    row = lambda r: (r, 0)
    staged = any(d > 1 for _, _, d in outs)
    return pl.pallas_call(
        functools.partial(_qkv_kernel, cq=cq, ck=ck, outs=tuple(outs)),
        grid=(t // tm,),
        in_specs=[
            pl.BlockSpec((tm, D_MODEL), row),
            _const_spec((1, D_MODEL)),
            _const_spec((D_MODEL, c)),
            _const_spec((1, cq)),
            _const_spec((1, ck)),
        ],
        out_specs=[pl.BlockSpec((tm // d, d * w_), row) for _, w_, d in outs],
        out_shape=[jax.ShapeDtypeStruct((t // d, d * w_), BF16) for _, w_, d in outs],
        scratch_shapes=[pltpu.VMEM((c // LANES, tm, LANES), F32)] if staged else [],
        compiler_params=pltpu.CompilerParams(
            dimension_semantics=("arbitrary",), vmem_limit_bytes=VMEM_LIMIT),
        name="qkv",
    )(x2d, gn, w, gq, gk)


def _attn_kernel(*refs, radius, bq, grp, ncol, has_sink, has_lse, heads_by_col):
    nsub = bq // SUB_ROWS
    wp = SUB_ROWS + 2 * radius
    g_rows = grp * SUB_ROWS
    it = iter(refs)
    fast_ref = next(it)
    hp_ref = next(it)
    q_ref = next(it)
    kp_ref, kc_ref, kn_ref, vp_ref, vc_ref, vn_ref = (next(it) for _ in range(6))
    tab_refs = [next(it) for _ in range(nsub)]
    o_ref = next(it)
    lse_ref = next(it) if has_lse else None

    head0 = pl.program_id(0) * (2 * grp) if heads_by_col else 0

    def body(fast):
        for col in range(ncol):
            one_column(fast, col)

    def one_column(fast, col):
        kcols = slice(col * LANES, (col + 1) * LANES)
        q0 = col * grp * LANES
        kwin = jnp.concatenate([kp_ref[0, :, kcols], kc_ref[0, :, kcols], kn_ref[0, :, kcols]], axis=0)
        vwin = jnp.concatenate([vp_ref[0, :, kcols], vc_ref[0, :, kcols], vn_ref[0, :, kcols]], axis=0)
        klane = lax.broadcasted_iota(jnp.int32, kwin.shape, 1)
        kzero = jnp.zeros_like(kwin)
        k_half = [jnp.where(klane < HEAD_DIM, kwin, kzero),
                  jnp.where(klane >= HEAD_DIM, kwin, kzero)]
        olane = lax.broadcasted_iota(jnp.int32, (SUB_ROWS, LANES), 1)
        ones = jnp.ones((wp, LANES), BF16)

        for t in range(nsub):
            r0 = t * SUB_ROWS
            qst = jnp.concatenate(
                [q_ref[0, r0:r0 + SUB_ROWS, q0 + a * LANES:q0 + (a + 1) * LANES] for a in range(grp)], axis=0)
            probs, dens, maxes = [], [], []
            for par in range(2):
                sc = lax.dot_general(qst, k_half[par][r0:r0 + wp], (((1,), (1,)), ((), ())),
                                     preferred_element_type=F32)
                if fast:
                    probs.append(jnp.exp2(sc + tab_refs[t][par * g_rows:(par + 1) * g_rows, :]).astype(BF16))
                    continue
                for a in range(grp):
                    h = par * grp + a
                    sa = sc[a * SUB_ROWS:(a + 1) * SUB_ROWS] + tab_refs[t][h * SUB_ROWS:(h + 1) * SUB_ROWS, :]
                    m = jnp.max(sa, axis=-1, keepdims=True)
                    if has_sink:
                        sink = hp_ref[1, head0 + h]
                        m = jnp.maximum(m, sink)
                    p = jnp.exp2(sa - m)
                    den = jnp.sum(p, axis=-1, keepdims=True)
                    if has_sink:
                        den = den + jnp.exp2(sink - m)
                    probs.append(p.astype(BF16))
                    dens.append(den)
                    maxes.append(m)
            vsub = vwin[r0:r0 + wp]
            if fast:
                vsub = jnp.concatenate([vsub, ones], axis=1)
            pv = jnp.dot(jnp.concatenate(probs, axis=0), vsub, preferred_element_type=F32)
            lse_acc = jnp.zeros((SUB_ROWS, LANES), F32) if has_lse else None
            for a in range(grp):
                halves = []
                for par in range(2):
                    h = par * grp + a
                    rows = slice(h * SUB_ROWS, (h + 1) * SUB_ROWS)
                    if fast:
                        den = pv[rows, LANES:]
                        if has_sink:
                            den = den + hp_ref[2, head0 + h]
                        num = pv[rows, :LANES]
                        shift = hp_ref[0, head0 + h]
                    else:
                        den, num, shift = dens[h], pv[rows], maxes[h]
                    halves.append(num / den)
                    if has_lse:
                        lse_acc = jnp.where(olane == h, LN2 * shift + jnp.log(den), lse_acc)
                o_ref[0, r0:r0 + SUB_ROWS, q0 + a * LANES:q0 + (a + 1) * LANES] = jnp.where(
                    olane < HEAD_DIM, halves[0], halves[1]).astype(BF16)
            if has_lse:
                lse_ref[0, r0:r0 + SUB_ROWS, kcols] = lse_acc

    @pl.when(fast_ref[0] != 0)
    def _():
        body(True)

    @pl.when(fast_ref[0] == 0)
    def _():
        body(False)


def _bias_tables(radius, stride, slopes, stab, grp):
    wp = SUB_ROWS + 2 * radius
    i = np.arange(SUB_ROWS)[:, None]
    j = np.arange(wp)[None, :]
    rel = j - radius - i
    band = np.abs(rel) <= radius
    tabs = []
    for code in range(4):
        valid = band
        if code & 1:
            valid = valid & (j >= radius)
        if code & 2:
            valid = valid & (j < SUB_ROWS + radius)
        tabs.append(np.where(valid, -float(stride) * LOG2E * np.abs(rel), MASK_BIAS))
    base = jnp.asarray(np.stack(tabs), dtype=F32)
    slabs = slopes.shape[0]
    full = base[:, None, None] * slopes[None, :, :, None, None] - stab[None, :, :, None, None]
    return full.reshape(4, slabs, 2 * grp * SUB_ROWS, wp)


def _softmax_params(q_gain_scaled, k_gain, sink2, n_heads):
    bound = HEAD_DIM * jnp.max(jnp.abs(q_gain_scaled)) * jnp.max(jnp.abs(k_gain))
    fast = bound <= FAST_SOFTMAX_MAX_BOUND
    sink2 = jnp.zeros((n_heads,), F32) if sink2 is None else sink2
    stab = jnp.where(fast, jnp.maximum(bound, sink2), 0.0)
    hp = jnp.stack([stab, sink2 - stab, jnp.exp2(sink2 - stab)])
    return fast.astype(jnp.int32).reshape(1), hp


def _attn_call(q, k, v, slopes, fast, hp, *, radius, stride, grp, has_sink, want_lse):
    n, length, _ = q.shape
    cols = k.shape[2] // LANES
    heads_by_col = has_sink
    bq = next(c for c in (512, 256, SUB_ROWS) if length % c == 0)
    ncol = 1 if heads_by_col else math.gcd(cols, max(1, 512 // bq))
    assert bq % radius == 0
    nsub = bq // SUB_ROWS
    wp = SUB_ROWS + 2 * radius
    g_rows = 2 * grp * SUB_ROWS
    nblk = length // bq
    halo_per_blk = bq // radius
    n_halo = length // radius
    total_sub = length // SUB_ROWS
    tables = _bias_tables(radius, stride, slopes, hp[0].reshape(slopes.shape), grp)

    def prev_map(c, b, i):
        return (b, jnp.maximum(i * halo_per_blk - 1, 0), c)

    def next_map(c, b, i):
        return (b, jnp.minimum((i + 1) * halo_per_blk, n_halo - 1), c)

    cur_map = lambda c, b, i: (b, i, c)

    def tab_map(t):
        def f(c, b, i):
            u = i * nsub + t
            code = (u == 0).astype(jnp.int32) + 2 * (u == total_sub - 1).astype(jnp.int32)
            return (code, c if heads_by_col else 0, 0, 0)
        return f

    smem = pl.BlockSpec(memory_space=pltpu.SMEM)
    in_specs = [smem, smem, pl.BlockSpec((1, bq, ncol * grp * LANES), cur_map)]
    args = [fast, hp, q]
    for arr in (k, v):
        in_specs += [pl.BlockSpec((1, radius, ncol * LANES), prev_map),
                     pl.BlockSpec((1, bq, ncol * LANES), cur_map),
                     pl.BlockSpec((1, radius, ncol * LANES), next_map)]
        args += [arr, arr, arr]
    for t in range(nsub):
        in_specs.append(pl.BlockSpec((None, None, g_rows, wp), tab_map(t)))
        args.append(tables)

    out_specs = [pl.BlockSpec((1, bq, ncol * grp * LANES), cur_map)]
    out_shape = [jax.ShapeDtypeStruct(q.shape, BF16)]
    if want_lse:
        out_specs.append(pl.BlockSpec((1, bq, ncol * LANES), cur_map))
        out_shape.append(jax.ShapeDtypeStruct((n, length, cols * LANES), F32))

    return pl.pallas_call(
        functools.partial(_attn_kernel, radius=radius, bq=bq, grp=grp, ncol=ncol, has_sink=has_sink,
                          has_lse=want_lse, heads_by_col=heads_by_col),
        grid=(cols // ncol, n, nblk),
        in_specs=in_specs,
        out_specs=out_specs,
        out_shape=out_shape,
        compiler_params=pltpu.CompilerParams(
            dimension_semantics=("arbitrary", "arbitrary", "arbitrary"),
            vmem_limit_bytes=VMEM_LIMIT),
        name="attn",
    )(*args)


def _ffn_chunks():
    chunks, c = [], 0
    while c < FFN_HIDDEN:
        w = min(2 * MXU_DIM, FFN_HIDDEN - c)
        chunks.append((c, w))
        c += w
    return chunks


def _gather_residues(src_ref, stage_ref, dil):
    if dil == 1:
        return src_ref[...].astype(F32)
    slabs, rows, _ = stage_ref.shape
    w = slabs * LANES
    for r in range(dil):
        for j in range(slabs):
            piece = src_ref[:, r * w + j * LANES:r * w + (j + 1) * LANES]
            stage_ref[j, pl.ds(r, rows // dil, stride=dil), :] = piece.astype(F32)
    parts = [stage_ref[j] for j in range(slabs)]
    return parts[0] if slabs == 1 else jnp.concatenate(parts, axis=1)


def _post_kernel(*refs, dils):
    n_groups = max(len(dils), 1)
    it = iter(refs)
    x_ref = next(it)
    o_refs = [next(it) for _ in range(n_groups)]
    lse_refs = [next(it) for _ in dils]
    expand_ref = next(it) if dils else None
    p_ref = next(it)
    wo_ref, nf_ref, wg_ref, wu_ref, wd_ref, np_ref, pg_ref, pp_ref = (next(it) for _ in range(8))
    out_ref = next(it)
    o_stage = {g: next(it) for g, d in enumerate(dils) if d > 1}
    l_stage = {g: next(it) for g, d in enumerate(dils) if d > 1}

    if not dils:
        o = o_refs[0][...]
    else:
        os_ = [_gather_residues(o_refs[g], o_stage.get(g), d) for g, d in enumerate(dils)]
        lses = [_gather_residues(lse_refs[g], l_stage.get(g), d) for g, d in enumerate(dils)]
        mx = functools.reduce(jnp.maximum, lses)
        es = [jnp.exp(l - mx) for l in lses]
        inv = 1.0 / functools.reduce(lambda a, b: a + b, es)
        expand = expand_ref[...]
        pieces = []
        for g in range(n_groups):
            alpha = es[g] * inv
            hi = alpha.astype(BF16)
            lo = (alpha - hi.astype(F32)).astype(BF16)
            spread = (jnp.dot(hi, expand, preferred_element_type=F32)
                      + jnp.dot(lo, expand, preferred_element_type=F32))
            pieces.append((os_[g] * spread).astype(BF16))
        o = jnp.concatenate(pieces, axis=1)

    x1 = x_ref[...] + jnp.dot(o, wo_ref[...], preferred_element_type=F32)
    h = _rmsnorm_rows(x1, nf_ref[...]).astype(BF16)
    acc = None
    for c, w in _ffn_chunks():
        g = jnp.dot(h, wg_ref[:, c:c + w], preferred_element_type=F32)
        u = jnp.dot(h, wu_ref[:, c:c + w], preferred_element_type=F32)
        act = (g * jax.nn.sigmoid(g) * u).astype(BF16)
        d = jnp.dot(act, wd_ref[c:c + w, :], preferred_element_type=F32)
        acc = d if acc is None else acc + d
    x2 = x1 + acc
    hp = _rmsnorm_rows(x2, np_ref[...]).astype(BF16)
    gate = jax.nn.sigmoid(jnp.dot(hp, pg_ref[...], preferred_element_type=F32))
    proj = jnp.dot(p_ref[...].astype(BF16), pp_ref[...], preferred_element_type=F32)
    out_ref[...] = x2 + gate * proj


def _head_expand_matrix():
    c = np.arange(B_GRP * LANES)
    head = ((c % LANES) // HEAD_DIM) * B_GRP + c // LANES
    return jnp.asarray((np.arange(LANES)[:, None] == head[None, :]).astype(np.float32), dtype=BF16)


def _post_call(x2d, os_, lses, dils, p3d, layer, wo, nf, wg, wu, wd, npl, pg, pp, *, tm):
    t = x2d.shape[0]
    row = lambda r: (r, 0)
    group_dils = dils if dils else (1,)
    in_specs = [pl.BlockSpec((tm, D_MODEL), row)]
    in_specs += [pl.BlockSpec((tm // d, o.shape[1]), row) for o, d in zip(os_, group_dils)]
    in_specs += [pl.BlockSpec((tm // d, l.shape[1]), row) for l, d in zip(lses, dils)]
    extra = []
    if dils:
        extra.append(_head_expand_matrix())
        in_specs.append(_const_spec(extra[0].shape))
    in_specs.append(pl.BlockSpec((None, tm, PLE_DIM), lambda r: (layer, r, 0)))
    weights = [wo, nf, wg, wu, wd, npl, pg, pp]
    in_specs += [_const_spec(w.shape) for w in weights]
    scratch = [pltpu.VMEM((B_GRP, tm, LANES), F32) for d in dils if d > 1]
    scratch += [pltpu.VMEM((1, tm, LANES), F32) for d in dils if d > 1]
    return pl.pallas_call(
        functools.partial(_post_kernel, dils=tuple(dils)),
        grid=(t // tm,),
        in_specs=in_specs,
        out_specs=pl.BlockSpec((tm, D_MODEL), row),
        out_shape=jax.ShapeDtypeStruct((t, D_MODEL), F32),
        scratch_shapes=scratch,
        compiler_params=pltpu.CompilerParams(
            dimension_semantics=("arbitrary",), vmem_limit_bytes=VMEM_LIMIT),
        name="post",
    )(x2d, *os_, *lses, *extra, p3d, *weights)


def _alibi_slopes(n):
    return 2.0 ** (-8.0 * np.arange(1, n + 1, dtype=np.float64) / n)


def _a_head_order():
    return [(2 * s + par) * A_GRP + a
            for s in range(A_KV_HEADS // 2) for a in range(A_GRP) for par in range(2)]


def _b_head_order():
    return [g * B_Q_PER_GROUP + par * B_GRP + a
            for g in range(len(B_GROUPS)) for a in range(B_GRP) for par in range(2)]


def _slab_param_order(n_slabs, grp):
    return [(s * grp + a) * 2 + par for s in range(n_slabs) for par in range(2) for a in range(grp)]


def _q_gain_scaled(q_gain):
    return q_gain * (HEAD_DIM ** -0.5 * LOG2E)


def _prep_qkv_weights(wqkv, wo, q_gain, k_gain, hq, hkv, order):
    cq, ck = hq * HEAD_DIM, hkv * HEAD_DIM
    order = np.asarray(order)
    wq = wqkv[:, :cq].reshape(D_MODEL, hq, HEAD_DIM)[:, order].reshape(D_MODEL, cq)
    w = jnp.concatenate([wq, wqkv[:, cq:]], axis=1).astype(BF16)
    wo_p = wo.reshape(hq, HEAD_DIM, D_MODEL)[order].reshape(cq, D_MODEL).astype(BF16)
    root = HEAD_DIM ** 0.5
    gq = jnp.tile(_q_gain_scaled(q_gain) * root, hq).reshape(1, cq)
    gk = jnp.tile(k_gain * root, hkv).reshape(1, ck)
    return w, wo_p, gq, gk


def _trunk(x, p, norm_mix, a_prep, b_prep, ffn_prep, *, tm):
    b, s, _ = x.shape
    t = b * s
    x2d = x.reshape(t, D_MODEL)
    p3d = p.reshape(p.shape[0], t, PLE_DIM)

    w, wo_p, gq, gk, slopes, fast, hp = a_prep
    cq, ck = A_Q_HEADS * HEAD_DIM, A_KV_HEADS * HEAD_DIM
    q, k, v = _qkv_call(x2d, norm_mix[0].reshape(1, -1), w, gq, gk, cq=cq, ck=ck,
                        outs=[(0, cq, 1), (cq, ck, 1), (cq + ck, ck, 1)], tm=2 * tm)
    o = _attn_call(q.reshape(b, s, cq), k.reshape(b, s, ck), v.reshape(b, s, ck), slopes, fast, hp,
                   radius=A_RADIUS, stride=1, grp=A_GRP, has_sink=True, want_lse=False)[0]
    x2d = _post_call(x2d, [o.reshape(t, cq)], [], (), p3d, 0, wo_p, *ffn_prep[0], tm=tm)

    w, wo_p, gq, gk, slopes, fast, hp = b_prep
    cq, ck = B_Q_HEADS * HEAD_DIM, B_KV_HEADS * HEAD_DIM
    gq_w = B_GRP * LANES
    dils = tuple(d for _, d in B_GROUPS)
    outs = []
    for g, d in enumerate(dils):
        outs += [(g * gq_w, gq_w, d), (cq + g * LANES, LANES, d), (cq + ck + g * LANES, LANES, d)]
    qkv = _qkv_call(x2d, norm_mix[1].reshape(1, -1), w, gq, gk, cq=cq, ck=ck, outs=outs, tm=2 * tm)
    os_, lses = [], []
    for g, (window, d) in enumerate(B_GROUPS):
        radius = window // (2 * d)
        qg, kg, vg = (a.reshape(b, s // d, a.shape[1]) for a in qkv[3 * g:3 * g + 3])
        og, lg = _attn_call(qg, kg, vg, slopes[g:g + 1], fast, hp[:, g * B_Q_PER_GROUP:(g + 1) * B_Q_PER_GROUP],
                            radius=radius, stride=d, grp=B_GRP, has_sink=False, want_lse=True)
        os_.append(og.reshape(t // d, d * gq_w))
        lses.append(lg.reshape(t // d, d * LANES))
    x2d = _post_call(x2d, os_, lses, dils, p3d, 1, wo_p, *ffn_prep[1], tm=tm)
    return x2d.reshape(b, s, D_MODEL)


def kernel(x_prompt, x_sample, p_prompt, p_sample, norm_mix, norm_ffn, norm_ple,
           a_wqkv, a_wo, a_q_gain, a_k_gain, a_sink,
           b_wqkv, b_wo, b_q_gain, b_k_gain,
           ffn_w_gate, ffn_w_up, ffn_w_down, ple_w_gate, ple_w_proj):
    a_order, b_order = _a_head_order(), _b_head_order()
    a_slabs = A_KV_HEADS // 2

    a_par = np.asarray(a_order)[_slab_param_order(a_slabs, A_GRP)]
    a_slopes = jnp.asarray(_alibi_slopes(A_Q_HEADS)[a_par].reshape(a_slabs, 2 * A_GRP), dtype=F32)
    a_prep = _prep_qkv_weights(a_wqkv[0], a_wo[0], a_q_gain[0], a_k_gain[0], A_Q_HEADS, A_KV_HEADS, a_order)
    a_prep += (a_slopes,) + _softmax_params(_q_gain_scaled(a_q_gain[0]), a_k_gain[0], a_sink[0][a_par] * LOG2E,
                                            A_Q_HEADS)

    b_par = np.asarray(b_order)[_slab_param_order(len(B_GROUPS), B_GRP)]
    b_slopes = jnp.asarray(_alibi_slopes(B_Q_HEADS)[b_par].reshape(len(B_GROUPS), 2 * B_GRP), dtype=F32)
    b_prep = _prep_qkv_weights(b_wqkv[0], b_wo[0], b_q_gain[0], b_k_gain[0], B_Q_HEADS, B_KV_HEADS, b_order)
    b_prep += (b_slopes,) + _softmax_params(_q_gain_scaled(b_q_gain[0]), b_k_gain[0], None, B_Q_HEADS)

    ffn_prep = []
    for i in range(norm_ffn.shape[0]):
        ffn_prep.append((norm_ffn[i].reshape(1, -1), ffn_w_gate[i].astype(BF16), ffn_w_up[i].astype(BF16),
                         ffn_w_down[i].astype(BF16), norm_ple[i].reshape(1, -1),
                         ple_w_gate[i].astype(BF16), ple_w_proj[i].astype(BF16)))

    run = functools.partial(_trunk, norm_mix=norm_mix, a_prep=a_prep, b_prep=b_prep,
                            ffn_prep=ffn_prep, tm=512)
    return (run(x_prompt, p_prompt), run(x_sample, p_sample))
```

```python
import functools
import math

import numpy as np
import jax
import jax.numpy as jnp
from jax import lax
from jax.experimental import pallas as pl
from jax.experimental.pallas import tpu as pltpu

D_MODEL = 1024
HEAD_DIM = 64
PLE_DIM = 256
FFN_HIDDEN = 2816
EPS = 1e-6

A_Q_HEADS = 16
A_KV_HEADS = 4
A_RADIUS = 128
B_GROUPS = ((128, 1), (512, 4), (2048, 16))
B_Q_PER_GROUP = 6
B_KV_PER_GROUP = 2
B_Q_HEADS = B_Q_PER_GROUP * len(B_GROUPS)
B_KV_HEADS = B_KV_PER_GROUP * len(B_GROUPS)
A_GRP = A_Q_HEADS // A_KV_HEADS
B_GRP = B_Q_PER_GROUP // B_KV_PER_GROUP

LANES = 128
MXU_DIM = 256
SUB_ROWS = 128
ATTN_STEP_ROWS = 1024
MASK_BIAS = -1e32
VMEM_LIMIT = 56 * 1024 * 1024
LOG2E = math.log2(math.e)
LN2 = math.log(2.0)
FAST_SOFTMAX_MAX_BOUND = 30.0 * LOG2E

F32 = jnp.float32
BF16 = jnp.bfloat16


def _const_spec(shape):
    nd = len(shape)
    return pl.BlockSpec(shape, lambda *_: (0,) * nd, pipeline_mode=pl.Buffered(1))


def _rmsnorm_rows(x, gain):
    ms = jnp.mean(x * x, axis=-1, keepdims=True)
    return x * lax.rsqrt(ms + EPS) * gain


def _head_sumsq(t):
    width = t.shape[1]
    lane = lax.broadcasted_iota(jnp.int32, (t.shape[0], LANES), 1)
    parts = []
    for c in range(0, width, LANES):
        blk = t[:, c:c + LANES]
        sq = blk * blk
        tot = jnp.sum(sq, axis=-1, keepdims=True)
        lo = jnp.sum(jnp.where(lane < HEAD_DIM, sq, 0.0), axis=-1, keepdims=True)
        parts.append(jnp.where(lane < HEAD_DIM, lo, tot - lo))
    return parts[0] if len(parts) == 1 else jnp.concatenate(parts, axis=1)


def _qkv_kernel(x_ref, gn_ref, w_ref, gq_ref, gk_ref, *rest, cq, ck, outs):
    out_refs = rest[:len(outs)]
    stage_ref = rest[len(outs)] if len(rest) > len(outs) else None
    tm = x_ref.shape[0]
    h = _rmsnorm_rows(x_ref[...], gn_ref[...]).astype(BF16)
    qkv = jnp.dot(h, w_ref[...], preferred_element_type=F32)
    q = qkv[:, :cq]
    k = qkv[:, cq:cq + ck]
    q = q * lax.rsqrt(_head_sumsq(q) + HEAD_DIM * EPS) * gq_ref[...]
    k = k * lax.rsqrt(_head_sumsq(k) + HEAD_DIM * EPS) * gk_ref[...]
    full = jnp.concatenate([q, k, qkv[:, cq + ck:]], axis=1)
    if stage_ref is not None:
        for j in range(stage_ref.shape[0]):
            stage_ref[j] = full[:, j * LANES:(j + 1) * LANES]
    for (off, w, dil), o_ref in zip(outs, out_refs):
        if dil == 1:
            o_ref[...] = full[:, off:off + w].astype(BF16)
        else:
            rows = tm // dil
            for r in range(dil):
                for j in range(w // LANES):
                    piece = stage_ref[off // LANES + j, pl.ds(r, rows, stride=dil), :]
                    o_ref[:, r * w + j * LANES:r * w + (j + 1) * LANES] = piece.astype(BF16)


def _qkv_call(x2d, gn, w, gq, gk, *, cq, ck, outs, tm):
    t = x2d.shape[0]
    c = cq + 2 * ck
    row = lambda r: (r, 0)
    staged = any(d > 1 for _, _, d in outs)
    return pl.pallas_call(
        functools.partial(_qkv_kernel, cq=cq, ck=ck, outs=tuple(outs)),
        grid=(t // tm,),
        in_specs=[
            pl.BlockSpec((tm, D_MODEL), row),
            _const_spec((1, D_MODEL)),
            _const_spec((D_MODEL, c)),
            _const_spec((1, cq)),
            _const_spec((1, ck)),
        ],
        out_specs=[pl.BlockSpec((tm // d, d * w_), row) for _, w_, d in outs],
        out_shape=[jax.ShapeDtypeStruct((t // d, d * w_), BF16) for _, w_, d in outs],
        scratch_shapes=[pltpu.VMEM((c // LANES, tm, LANES), F32)] if staged else [],
        compiler_params=pltpu.CompilerParams(
            dimension_semantics=("arbitrary",), vmem_limit_bytes=VMEM_LIMIT),
        name="qkv",
    )(x2d, gn, w, gq, gk)


def _attn_kernel(*refs, radius, bq, grp, ncol, has_sink, has_lse, heads_by_col):
    nsub = bq // SUB_ROWS
    wp = SUB_ROWS + 2 * radius
    g_rows = grp * SUB_ROWS
    it = iter(refs)
    fast_ref = next(it)
    hp_ref = next(it)
    q_ref = next(it)
    kp_ref, kc_ref, kn_ref, vp_ref, vc_ref, vn_ref = (next(it) for _ in range(6))
    tab_refs = [next(it) for _ in range(min(nsub, 3))]
    o_ref = next(it)
    lse_ref = next(it) if has_lse else None

    head0 = pl.program_id(0) * (2 * grp) if heads_by_col else 0

    def table(t):
        if t == 0:
            return tab_refs[0]
        return tab_refs[-1] if t == nsub - 1 else tab_refs[1]

    def body(fast):
        for col in range(ncol):
            one_column(fast, col)

    def one_column(fast, col):
        kcols = slice(col * LANES, (col + 1) * LANES)
        q0 = col * grp * LANES
        kwin = jnp.concatenate([kp_ref[0, :, kcols], kc_ref[0, :, kcols], kn_ref[0, :, kcols]], axis=0)
        vwin = jnp.concatenate([vp_ref[0, :, kcols], vc_ref[0, :, kcols], vn_ref[0, :, kcols]], axis=0)
        klane = lax.broadcasted_iota(jnp.int32, kwin.shape, 1)
        kzero = jnp.zeros_like(kwin)
        k_half = [jnp.where(klane < HEAD_DIM, kwin, kzero),
                  jnp.where(klane >= HEAD_DIM, kwin, kzero)]
        olane = lax.broadcasted_iota(jnp.int32, (SUB_ROWS, LANES), 1)
        ones = jnp.ones((wp, LANES), BF16)

        for t in range(nsub):
            r0 = t * SUB_ROWS
            qst = jnp.concatenate(
                [q_ref[0, r0:r0 + SUB_ROWS, q0 + a * LANES:q0 + (a + 1) * LANES] for a in range(grp)], axis=0)
            probs, dens, maxes = [], [], []
            for par in range(2):
                sc = lax.dot_general(qst, k_half[par][r0:r0 + wp], (((1,), (1,)), ((), ())),
                                     preferred_element_type=F32)
                if fast:
                    probs.append(jnp.exp2(sc + table(t)[par * g_rows:(par + 1) * g_rows, :]).astype(BF16))
                    continue
                for a in range(grp):
                    h = par * grp + a
                    sa = sc[a * SUB_ROWS:(a + 1) * SUB_ROWS] + table(t)[h * SUB_ROWS:(h + 1) * SUB_ROWS, :]
                    m = jnp.max(sa, axis=-1, keepdims=True)
                    if has_sink:
                        sink = hp_ref[1, head0 + h]
                        m = jnp.maximum(m, sink)
                    p = jnp.exp2(sa - m)
                    den = jnp.sum(p, axis=-1, keepdims=True)
                    if has_sink:
                        den = den + jnp.exp2(sink - m)
                    probs.append(p.astype(BF16))
                    dens.append(den)
                    maxes.append(m)
            vsub = vwin[r0:r0 + wp]
            if fast:
                vsub = jnp.concatenate([vsub, ones], axis=1)
            pv = jnp.dot(jnp.concatenate(probs, axis=0), vsub, preferred_element_type=F32)
            lse_acc = jnp.zeros((SUB_ROWS, LANES), F32) if has_lse else None
            for a in range(grp):
                halves = []
                for par in range(2):
                    h = par * grp + a
                    rows = slice(h * SUB_ROWS, (h + 1) * SUB_ROWS)
                    if fast:
                        den = pv[rows, LANES:]
                        if has_sink:
                            den = den + hp_ref[2, head0 + h]
                        num = pv[rows, :LANES]
                        shift = hp_ref[0, head0 + h]
                    else:
                        den, num, shift = dens[h], pv[rows], maxes[h]
                    halves.append(num / den)
                    if has_lse:
                        lse_acc = jnp.where(olane == h, LN2 * shift + jnp.log(den), lse_acc)
                o_ref[0, r0:r0 + SUB_ROWS, q0 + a * LANES:q0 + (a + 1) * LANES] = jnp.where(
                    olane < HEAD_DIM, halves[0], halves[1]).astype(BF16)
            if has_lse:
                lse_ref[0, r0:r0 + SUB_ROWS, kcols] = lse_acc

    @pl.when(fast_ref[0] != 0)
    def _():
        body(True)

    @pl.when(fast_ref[0] == 0)
    def _():
        body(False)


def _bias_tables(radius, stride, slopes, stab, grp):
    wp = SUB_ROWS + 2 * radius
    i = np.arange(SUB_ROWS)[:, None]
    j = np.arange(wp)[None, :]
    rel = j - radius - i
    band = np.abs(rel) <= radius
    tabs = []
    for code in range(4):
        valid = band
        if code & 1:
            valid = valid & (j >= radius)
        if code & 2:
            valid = valid & (j < SUB_ROWS + radius)
        tabs.append(np.where(valid, -float(stride) * LOG2E * np.abs(rel), MASK_BIAS))
    base = jnp.asarray(np.stack(tabs), dtype=F32)
    slabs = slopes.shape[0]
    full = base[:, None, None] * slopes[None, :, :, None, None] - stab[None, :, :, None, None]
    return full.reshape(4, slabs, 2 * grp * SUB_ROWS, wp)


def _softmax_params(q_gain_scaled, k_gain, sink2, n_heads):
    bound = HEAD_DIM * jnp.max(jnp.abs(q_gain_scaled)) * jnp.max(jnp.abs(k_gain))
    fast = bound <= FAST_SOFTMAX_MAX_BOUND
    sink2 = jnp.zeros((n_heads,), F32) if sink2 is None else sink2
    stab = jnp.where(fast, jnp.maximum(bound, sink2), 0.0)
    hp = jnp.stack([stab, sink2 - stab, jnp.exp2(sink2 - stab)])
    return fast.astype(jnp.int32).reshape(1), hp


def _attn_call(q, k, v, slopes, fast, hp, *, radius, stride, grp, has_sink, want_lse):
    n, length, _ = q.shape
    cols = k.shape[2] // LANES
    heads_by_col = has_sink
    bq = next(c for c in (ATTN_STEP_ROWS, 512, 256, SUB_ROWS) if length % c == 0)
    ncol = 1 if heads_by_col else math.gcd(cols, max(1, ATTN_STEP_ROWS // bq))
    assert bq % radius == 0
    nsub = bq // SUB_ROWS
    wp = SUB_ROWS + 2 * radius
    g_rows = 2 * grp * SUB_ROWS
    nblk = length // bq
    halo_per_blk = bq // radius
    n_halo = length // radius
    tables = _bias_tables(radius, stride, slopes, hp[0].reshape(slopes.shape), grp)

    def prev_map(c, b, i):
        return (b, jnp.maximum(i * halo_per_blk - 1, 0), c)

    def next_map(c, b, i):
        return (b, jnp.minimum((i + 1) * halo_per_blk, n_halo - 1), c)

    cur_map = lambda c, b, i: (b, i, c)

    def tab_map(first, last):
        def f(c, b, i):
            code = jnp.int32(0)
            if first:
                code = code + (i == 0).astype(jnp.int32)
            if last:
                code = code + 2 * (i == nblk - 1).astype(jnp.int32)
            return (code, c if heads_by_col else 0, 0, 0)
        return f

    tab_maps = [tab_map(True, True)] if nsub == 1 else (
        [tab_map(True, False)] + [tab_map(False, False)] * (nsub > 2) + [tab_map(False, True)])

    smem = pl.BlockSpec(memory_space=pltpu.SMEM)
    in_specs = [smem, smem, pl.BlockSpec((1, bq, ncol * grp * LANES), cur_map)]
    args = [fast, hp, q]
    for arr in (k, v):
        in_specs += [pl.BlockSpec((1, radius, ncol * LANES), prev_map),
                     pl.BlockSpec((1, bq, ncol * LANES), cur_map),
                     pl.BlockSpec((1, radius, ncol * LANES), next_map)]
        args += [arr, arr, arr]
    for index_map in tab_maps:
        in_specs.append(pl.BlockSpec((None, None, g_rows, wp), index_map))
        args.append(tables)

    out_specs = [pl.BlockSpec((1, bq, ncol * grp * LANES), cur_map)]
    out_shape = [jax.ShapeDtypeStruct(q.shape, BF16)]
    if want_lse:
        out_specs.append(pl.BlockSpec((1, bq, ncol * LANES), cur_map))
        out_shape.append(jax.ShapeDtypeStruct((n, length, cols * LANES), F32))

    return pl.pallas_call(
        functools.partial(_attn_kernel, radius=radius, bq=bq, grp=grp, ncol=ncol, has_sink=has_sink,
                          has_lse=want_lse, heads_by_col=heads_by_col),
        grid=(cols // ncol, n, nblk),
        in_specs=in_specs,
        out_specs=out_specs,
        out_shape=out_shape,
        compiler_params=pltpu.CompilerParams(
            dimension_semantics=("arbitrary", "arbitrary", "arbitrary"),
            vmem_limit_bytes=VMEM_LIMIT),
        name="attn",
    )(*args)


def _ffn_chunks():
    chunks, c = [], 0
    while c < FFN_HIDDEN:
        w = min(MXU_DIM, FFN_HIDDEN - c)
        chunks.append((c, w))
        c += w
    return chunks


def _gather_residues(src_ref, stage_ref, dil):
    if dil == 1:
        return src_ref[...].astype(F32)
    slabs, rows, _ = stage_ref.shape
    w = slabs * LANES
    for r in range(dil):
        for j in range(slabs):
            piece = src_ref[:, r * w + j * LANES:r * w + (j + 1) * LANES]
            stage_ref[j, pl.ds(r, rows // dil, stride=dil), :] = piece.astype(F32)
    parts = [stage_ref[j] for j in range(slabs)]
    return parts[0] if slabs == 1 else jnp.concatenate(parts, axis=1)


def _post_kernel(*refs, dils):
    n_groups = max(len(dils), 1)
    it = iter(refs)
    x_ref = next(it)
    o_refs = [next(it) for _ in range(n_groups)]
    lse_refs = [next(it) for _ in dils]
    expand_ref = next(it) if dils else None
    p_ref = next(it)
    wo_ref, nf_ref, wg_ref, wu_ref, wd_ref, np_ref, pg_ref, pp_ref = (next(it) for _ in range(8))
    out_ref = next(it)
    o_stage = {g: next(it) for g, d in enumerate(dils) if d > 1}
    l_stage = {g: next(it) for g, d in enumerate(dils) if d > 1}

    if not dils:
        o = o_refs[0][...]
    else:
        os_ = [_gather_residues(o_refs[g], o_stage.get(g), d) for g, d in enumerate(dils)]
        lses = [_gather_residues(lse_refs[g], l_stage.get(g), d) for g, d in enumerate(dils)]
        mx = functools.reduce(jnp.maximum, lses)
        es = [jnp.exp(l - mx) for l in lses]
        inv = 1.0 / functools.reduce(lambda a, b: a + b, es)
        expand = expand_ref[...]
        pieces = []
        for g in range(n_groups):
            alpha = es[g] * inv
            hi = alpha.astype(BF16)
            lo = (alpha - hi.astype(F32)).astype(BF16)
            spread = (jnp.dot(hi, expand, preferred_element_type=F32)
                      + jnp.dot(lo, expand, preferred_element_type=F32))
            pieces.append((os_[g] * spread).astype(BF16))
        o = jnp.concatenate(pieces, axis=1)

    x1 = x_ref[...] + jnp.dot(o, wo_ref[...], preferred_element_type=F32)
    h = _rmsnorm_rows(x1, nf_ref[...]).astype(BF16)
    acc = None
    for c, w in _ffn_chunks():
        g = jnp.dot(h, wg_ref[:, c:c + w], preferred_element_type=F32)
        u = jnp.dot(h, wu_ref[:, c:c + w], preferred_element_type=F32)
        act = (g * jax.nn.sigmoid(g) * u).astype(BF16)
        d = jnp.dot(act, wd_ref[c:c + w, :], preferred_element_type=F32)
        acc = d if acc is None else acc + d
    x2 = x1 + acc
    hp = _rmsnorm_rows(x2, np_ref[...]).astype(BF16)
    gate = jax.nn.sigmoid(jnp.dot(hp, pg_ref[...], preferred_element_type=F32))
    proj = jnp.dot(p_ref[...].astype(BF16), pp_ref[...], preferred_element_type=F32)
    out_ref[...] = x2 + gate * proj


def _head_expand_matrix():
    c = np.arange(B_GRP * LANES)
    head = ((c % LANES) // HEAD_DIM) * B_GRP + c // LANES
    return jnp.asarray((np.arange(LANES)[:, None] == head[None, :]).astype(np.float32), dtype=BF16)


def _post_call(x2d, os_, lses, dils, p3d, layer, wo, nf, wg, wu, wd, npl, pg, pp, *, tm):
    t = x2d.shape[0]
    row = lambda r: (r, 0)
    group_dils = dils if dils else (1,)
    in_specs = [pl.BlockSpec((tm, D_MODEL), row)]
    in_specs += [pl.BlockSpec((tm // d, o.shape[1]), row) for o, d in zip(os_, group_dils)]
    in_specs += [pl.BlockSpec((tm // d, l.shape[1]), row) for l, d in zip(lses, dils)]
    extra = []
    if dils:
        extra.append(_head_expand_matrix())
        in_specs.append(_const_spec(extra[0].shape))
    in_specs.append(pl.BlockSpec((None, tm, PLE_DIM), lambda r: (layer, r, 0)))
    weights = [wo, nf, wg, wu, wd, npl, pg, pp]
    in_specs += [_const_spec(w.shape) for w in weights]
    scratch = [pltpu.VMEM((B_GRP, tm, LANES), F32) for d in dils if d > 1]
    scratch += [pltpu.VMEM((1, tm, LANES), F32) for d in dils if d > 1]
    return pl.pallas_call(
        functools.partial(_post_kernel, dils=tuple(dils)),
        grid=(t // tm,),
        in_specs=in_specs,
        out_specs=pl.BlockSpec((tm, D_MODEL), row),
        out_shape=jax.ShapeDtypeStruct((t, D_MODEL), F32),
        scratch_shapes=scratch,
        compiler_params=pltpu.CompilerParams(
            dimension_semantics=("arbitrary",), vmem_limit_bytes=VMEM_LIMIT),
        name="post",
    )(x2d, *os_, *lses, *extra, p3d, *weights)


def _alibi_slopes(n):
    return 2.0 ** (-8.0 * np.arange(1, n + 1, dtype=np.float64) / n)


def _a_head_order():
    return [(2 * s + par) * A_GRP + a
            for s in range(A_KV_HEADS // 2) for a in range(A_GRP) for par in range(2)]


def _b_head_order():
    return [g * B_Q_PER_GROUP + par * B_GRP + a
            for g in range(len(B_GROUPS)) for a in range(B_GRP) for par in range(2)]


def _slab_param_order(n_slabs, grp):
    return [(s * grp + a) * 2 + par for s in range(n_slabs) for par in range(2) for a in range(grp)]


def _q_gain_scaled(q_gain):
    return q_gain * (HEAD_DIM ** -0.5 * LOG2E)


def _prep_qkv_weights(wqkv, wo, q_gain, k_gain, hq, hkv, order):
    cq, ck = hq * HEAD_DIM, hkv * HEAD_DIM
    order = np.asarray(order)
    wq = wqkv[:, :cq].reshape(D_MODEL, hq, HEAD_DIM)[:, order].reshape(D_MODEL, cq)
    w = jnp.concatenate([wq, wqkv[:, cq:]], axis=1).astype(BF16)
    wo_p = wo.reshape(hq, HEAD_DIM, D_MODEL)[order].reshape(cq, D_MODEL).astype(BF16)
    root = HEAD_DIM ** 0.5
    gq = jnp.tile(_q_gain_scaled(q_gain) * root, hq).reshape(1, cq)
    gk = jnp.tile(k_gain * root, hkv).reshape(1, ck)
    return w, wo_p, gq, gk


def _trunk(x, p, norm_mix, a_prep, b_prep, ffn_prep, *, tm):
    b, s, _ = x.shape
    t = b * s
    x2d = x.reshape(t, D_MODEL)
    p3d = p.reshape(p.shape[0], t, PLE_DIM)

    w, wo_p, gq, gk, slopes, fast, hp = a_prep
    cq, ck = A_Q_HEADS * HEAD_DIM, A_KV_HEADS * HEAD_DIM
    q, k, v = _qkv_call(x2d, norm_mix[0].reshape(1, -1), w, gq, gk, cq=cq, ck=ck,
                        outs=[(0, cq, 1), (cq, ck, 1), (cq + ck, ck, 1)], tm=2 * tm)
    o = _attn_call(q.reshape(b, s, cq), k.reshape(b, s, ck), v.reshape(b, s, ck), slopes, fast, hp,
                   radius=A_RADIUS, stride=1, grp=A_GRP, has_sink=True, want_lse=False)[0]
    x2d = _post_call(x2d, [o.reshape(t, cq)], [], (), p3d, 0, wo_p, *ffn_prep[0], tm=tm)

    w, wo_p, gq, gk, slopes, fast, hp = b_prep
    cq, ck = B_Q_HEADS * HEAD_DIM, B_KV_HEADS * HEAD_DIM
    gq_w = B_GRP * LANES
    dils = tuple(d for _, d in B_GROUPS)
    outs = []
    for g, d in enumerate(dils):
        outs += [(g * gq_w, gq_w, d), (cq + g * LANES, LANES, d), (cq + ck + g * LANES, LANES, d)]
    qkv = _qkv_call(x2d, norm_mix[1].reshape(1, -1), w, gq, gk, cq=cq, ck=ck, outs=outs, tm=2 * tm)
    os_, lses = [], []
    for g, (window, d) in enumerate(B_GROUPS):
        radius = window // (2 * d)
        qg, kg, vg = (a.reshape(b, s // d, a.shape[1]) for a in qkv[3 * g:3 * g + 3])
        og, lg = _attn_call(qg, kg, vg, slopes[g:g + 1], fast, hp[:, g * B_Q_PER_GROUP:(g + 1) * B_Q_PER_GROUP],
                            radius=radius, stride=d, grp=B_GRP, has_sink=False, want_lse=True)
        os_.append(og.reshape(t // d, d * gq_w))
        lses.append(lg.reshape(t // d, d * LANES))
    x2d = _post_call(x2d, os_, lses, dils, p3d, 1, wo_p, *ffn_prep[1], tm=tm)
    return x2d.reshape(b, s, D_MODEL)


def kernel(x_prompt, x_sample, p_prompt, p_sample, norm_mix, norm_ffn, norm_ple,
           a_wqkv, a_wo, a_q_gain, a_k_gain, a_sink,
           b_wqkv, b_wo, b_q_gain, b_k_gain,
           ffn_w_gate, ffn_w_up, ffn_w_down, ple_w_gate, ple_w_proj):
    a_order, b_order = _a_head_order(), _b_head_order()
    a_slabs = A_KV_HEADS // 2

    a_par = np.asarray(a_order)[_slab_param_order(a_slabs, A_GRP)]
    a_slopes = jnp.asarray(_alibi_slopes(A_Q_HEADS)[a_par].reshape(a_slabs, 2 * A_GRP), dtype=F32)
    a_prep = _prep_qkv_weights(a_wqkv[0], a_wo[0], a_q_gain[0], a_k_gain[0], A_Q_HEADS, A_KV_HEADS, a_order)
    a_prep += (a_slopes,) + _softmax_params(_q_gain_scaled(a_q_gain[0]), a_k_gain[0], a_sink[0][a_par] * LOG2E,
                                            A_Q_HEADS)

    b_par = np.asarray(b_order)[_slab_param_order(len(B_GROUPS), B_GRP)]
    b_slopes = jnp.asarray(_alibi_slopes(B_Q_HEADS)[b_par].reshape(len(B_GROUPS), 2 * B_GRP), dtype=F32)
    b_prep = _prep_qkv_weights(b_wqkv[0], b_wo[0], b_q_gain[0], b_k_gain[0], B_Q_HEADS, B_KV_HEADS, b_order)
    b_prep += (b_slopes,) + _softmax_params(_q_gain_scaled(b_q_gain[0]), b_k_gain[0], None, B_Q_HEADS)

    ffn_prep = []
    for i in range(norm_ffn.shape[0]):
        ffn_prep.append((norm_ffn[i].reshape(1, -1), ffn_w_gate[i].astype(BF16), ffn_w_up[i].astype(BF16),
                         ffn_w_down[i].astype(BF16), norm_ple[i].reshape(1, -1),
                         ple_w_gate[i].astype(BF16), ple_w_proj[i].astype(BF16)))

    run = functools.partial(_trunk, norm_mix=norm_mix, a_prep=a_prep, b_prep=b_prep,
                            ffn_prep=ffn_prep, tm=512)
    return (run(x_prompt, p_prompt), run(x_sample, p_sample))
```

```python
import functools
import math

import numpy as np
import jax
import jax.numpy as jnp
from jax import lax
from jax.experimental import pallas as pl
from jax.experimental.pallas import tpu as pltpu

D_MODEL = 1024
HEAD_DIM = 64
PLE_DIM = 256
FFN_HIDDEN = 2816
EPS = 1e-6

A_Q_HEADS = 16
A_KV_HEADS = 4
A_RADIUS = 128
B_GROUPS = ((128, 1), (512, 4), (2048, 16))
B_Q_PER_GROUP = 6
B_KV_PER_GROUP = 2
B_Q_HEADS = B_Q_PER_GROUP * len(B_GROUPS)
B_KV_HEADS = B_KV_PER_GROUP * len(B_GROUPS)
A_GRP = A_Q_HEADS // A_KV_HEADS
B_GRP = B_Q_PER_GROUP // B_KV_PER_GROUP

LANES = 128
MXU_DIM = 256
SUB_ROWS = 128
ATTN_STEP_ROWS = 1024
MASK_BIAS = -1e32
VMEM_LIMIT = 56 * 1024 * 1024
LOG2E = math.log2(math.e)
LN2 = math.log(2.0)
FAST_SOFTMAX_MAX_BOUND = 30.0 * LOG2E

F32 = jnp.float32
BF16 = jnp.bfloat16


def _const_spec(shape):
    nd = len(shape)
    return pl.BlockSpec(shape, lambda *_: (0,) * nd, pipeline_mode=pl.Buffered(1))


def _layer_spec(shape, layer):
    return pl.BlockSpec((None,) + tuple(shape[1:]), lambda *_: (layer, 0, 0), pipeline_mode=pl.Buffered(1))


def _rmsnorm_rows(x, gain):
    ms = jnp.mean(x * x, axis=-1, keepdims=True)
    return x * lax.rsqrt(ms + EPS) * gain


def _head_sumsq(t):
    width = t.shape[1]
    lane = lax.broadcasted_iota(jnp.int32, (t.shape[0], LANES), 1)
    parts = []
    for c in range(0, width, LANES):
        blk = t[:, c:c + LANES]
        sq = blk * blk
        tot = jnp.sum(sq, axis=-1, keepdims=True)
        lo = jnp.sum(jnp.where(lane < HEAD_DIM, sq, 0.0), axis=-1, keepdims=True)
        parts.append(jnp.where(lane < HEAD_DIM, lo, tot - lo))
    return parts[0] if len(parts) == 1 else jnp.concatenate(parts, axis=1)


def _qkv_kernel(x_ref, gn_ref, w_ref, gq_ref, gk_ref, *rest, cq, ck, outs):
    out_refs = rest[:len(outs)]
    stage_ref = rest[len(outs)] if len(rest) > len(outs) else None
    tm = x_ref.shape[0]
    h = _rmsnorm_rows(x_ref[...], gn_ref[...]).astype(BF16)
    qkv = jnp.dot(h, w_ref[...], preferred_element_type=F32)
    q = qkv[:, :cq]
    k = qkv[:, cq:cq + ck]
    q = q * lax.rsqrt(_head_sumsq(q) + HEAD_DIM * EPS) * gq_ref[...]
    k = k * lax.rsqrt(_head_sumsq(k) + HEAD_DIM * EPS) * gk_ref[...]
    full = jnp.concatenate([q, k, qkv[:, cq + ck:]], axis=1)
    if stage_ref is not None:
        for j in range(stage_ref.shape[0]):
            stage_ref[j] = full[:, j * LANES:(j + 1) * LANES]
    for (off, w, dil), o_ref in zip(outs, out_refs):
        if dil == 1:
            o_ref[...] = full[:, off:off + w].astype(BF16)
        else:
            rows = tm // dil
            for r in range(dil):
                for j in range(w // LANES):
                    piece = stage_ref[off // LANES + j, pl.ds(r, rows, stride=dil), :]
                    o_ref[:, r * w + j * LANES:r * w + (j + 1) * LANES] = piece.astype(BF16)


def _qkv_call(x2d, gn, w, gq, gk, *, cq, ck, outs, tm):
    t = x2d.shape[0]
    c = cq + 2 * ck
    row = lambda r: (r, 0)
    staged = any(d > 1 for _, _, d in outs)
    return pl.pallas_call(
        functools.partial(_qkv_kernel, cq=cq, ck=ck, outs=tuple(outs)),
        grid=(t // tm,),
        in_specs=[
            pl.BlockSpec((tm, D_MODEL), row),
            _const_spec((1, D_MODEL)),
            _const_spec((D_MODEL, c)),
            _const_spec((1, cq)),
            _const_spec((1, ck)),
        ],
        out_specs=[pl.BlockSpec((tm // d, d * w_), row) for _, w_, d in outs],
        out_shape=[jax.ShapeDtypeStruct((t // d, d * w_), BF16) for _, w_, d in outs],
        scratch_shapes=[pltpu.VMEM((c // LANES, tm, LANES), F32)] if staged else [],
        compiler_params=pltpu.CompilerParams(
            dimension_semantics=("arbitrary",), vmem_limit_bytes=VMEM_LIMIT),
        name="qkv",
    )(x2d, gn, w, gq, gk)


def _attn_kernel(*refs, radius, bq, grp, ncol, has_sink, has_lse, heads_by_col):
    nsub = bq // SUB_ROWS
    wp = SUB_ROWS + 2 * radius
    it = iter(refs)
    fast_ref = next(it)
    hp_ref = next(it)
    q_ref = next(it)
    kp_ref, kc_ref, kn_ref, vp_ref, vc_ref, vn_ref = (next(it) for _ in range(6))
    tab_refs = [next(it) for _ in range(min(nsub, 3))]
    o_ref = next(it)
    lse_ref = next(it) if has_lse else None

    head0 = pl.program_id(0) * (2 * grp) if heads_by_col else 0

    def table(refs, t):
        if t == 0:
            return refs[0]
        return refs[-1] if t == nsub - 1 else refs[1]

    def body(fast):
        for col in range(ncol):
            kcols = slice(col * LANES, (col + 1) * LANES)
            kwin = jnp.concatenate([kp_ref[0, :, kcols], kc_ref[0, :, kcols], kn_ref[0, :, kcols]], axis=0)
            vwin = jnp.concatenate([vp_ref[0, :, kcols], vc_ref[0, :, kcols], vn_ref[0, :, kcols]], axis=0)
            klane = lax.broadcasted_iota(jnp.int32, kwin.shape, 1)
            kzero = jnp.zeros_like(kwin)
            k_half = [jnp.where(klane < HEAD_DIM, kwin, kzero),
                      jnp.where(klane >= HEAD_DIM, kwin, kzero)]
            (bound_softmax_column if fast else rowmax_softmax_column)(col, k_half, vwin)

    def stacked_q(col, t):
        q0 = col * grp * LANES
        return jnp.concatenate(
            [q_ref[0, t * SUB_ROWS:(t + 1) * SUB_ROWS, q0 + a * LANES:q0 + (a + 1) * LANES]
             for a in range(grp)], axis=0)

    def bound_softmax_column(col, k_half, vwin):
        kcols = slice(col * LANES, (col + 1) * LANES)
        q0 = col * grp * LANES
        g_rows = grp * SUB_ROWS
        olane = lax.broadcasted_iota(jnp.int32, (SUB_ROWS, LANES), 1)
        ones = jnp.ones((wp, LANES), BF16)
        for t in range(nsub):
            r0 = t * SUB_ROWS
            qst = stacked_q(col, t)
            probs = []
            for par in range(2):
                sc = lax.dot_general(qst, k_half[par][r0:r0 + wp], (((1,), (1,)), ((), ())),
                                     preferred_element_type=F32)
                probs.append(jnp.exp2(sc + table(tab_refs, t)[par * g_rows:(par + 1) * g_rows, :]).astype(BF16))
            pv = jnp.dot(jnp.concatenate(probs, axis=0), jnp.concatenate([vwin[r0:r0 + wp], ones], axis=1),
                         preferred_element_type=F32)
            lse_acc = jnp.zeros((SUB_ROWS, LANES), F32) if has_lse else None
            for a in range(grp):
                halves = []
                for par in range(2):
                    h = par * grp + a
                    rows = slice(h * SUB_ROWS, (h + 1) * SUB_ROWS)
                    den = pv[rows, LANES:]
                    if has_sink:
                        den = den + hp_ref[2, head0 + h]
                    halves.append(pv[rows, :LANES] / den)
                    if has_lse:
                        lse_acc = jnp.where(olane == h, LN2 * hp_ref[0, head0 + h] + jnp.log(den), lse_acc)
                o_ref[0, r0:r0 + SUB_ROWS, q0 + a * LANES:q0 + (a + 1) * LANES] = jnp.where(
                    olane < HEAD_DIM, halves[0], halves[1]).astype(BF16)
            if has_lse:
                lse_ref[0, r0:r0 + SUB_ROWS, kcols] = lse_acc

    def rowmax_softmax_column(col, k_half, vwin):
        kcols = slice(col * LANES, (col + 1) * LANES)
        q0 = col * grp * LANES
        olane = lax.broadcasted_iota(jnp.int32, (SUB_ROWS, LANES), 1)
        for t in range(nsub):
            r0 = t * SUB_ROWS
            qst = stacked_q(col, t)
            probs, dens, maxes = [], [], []
            for par in range(2):
                sc = lax.dot_general(qst, k_half[par][r0:r0 + wp], (((1,), (1,)), ((), ())),
                                     preferred_element_type=F32)
                for a in range(grp):
                    h = par * grp + a
                    sa = sc[a * SUB_ROWS:(a + 1) * SUB_ROWS] + table(tab_refs, t)[h * SUB_ROWS:(h + 1) * SUB_ROWS, :]
                    m = jnp.max(sa, axis=-1, keepdims=True)
                    if has_sink:
                        sink = hp_ref[1, head0 + h]
                        m = jnp.maximum(m, sink)
                    p = jnp.exp2(sa - m)
                    den = jnp.sum(p, axis=-1, keepdims=True)
                    if has_sink:
                        den = den + jnp.exp2(sink - m)
                    probs.append(p.astype(BF16))
                    dens.append(den)
                    maxes.append(m)
            pv = jnp.dot(jnp.concatenate(probs, axis=0), vwin[r0:r0 + wp], preferred_element_type=F32)
            lse_acc = jnp.zeros((SUB_ROWS, LANES), F32) if has_lse else None
            for a in range(grp):
                halves = []
                for par in range(2):
                    h = par * grp + a
                    halves.append(pv[h * SUB_ROWS:(h + 1) * SUB_ROWS] / dens[h])
                    if has_lse:
                        lse_acc = jnp.where(olane == h, LN2 * maxes[h] + jnp.log(dens[h]), lse_acc)
                o_ref[0, r0:r0 + SUB_ROWS, q0 + a * LANES:q0 + (a + 1) * LANES] = jnp.where(
                    olane < HEAD_DIM, halves[0], halves[1]).astype(BF16)
            if has_lse:
                lse_ref[0, r0:r0 + SUB_ROWS, kcols] = lse_acc

    @pl.when(fast_ref[0] != 0)
    def _():
        body(True)

    @pl.when(fast_ref[0] == 0)
    def _():
        body(False)


def _bias_tables(radius, stride, slopes, stab, grp):
    wp = SUB_ROWS + 2 * radius
    i = np.arange(SUB_ROWS)[:, None]
    j = np.arange(wp)[None, :]
    rel = j - radius - i
    band = np.abs(rel) <= radius
    tabs = []
    for code in range(4):
        valid = band
        if code & 1:
            valid = valid & (j >= radius)
        if code & 2:
            valid = valid & (j < SUB_ROWS + radius)
        tabs.append(np.where(valid, -float(stride) * LOG2E * np.abs(rel), MASK_BIAS))
    base = jnp.asarray(np.stack(tabs), dtype=F32)
    slabs = slopes.shape[0]
    full = base[:, None, None] * slopes[None, :, :, None, None] - stab[None, :, :, None, None]
    return full.reshape(4, slabs, 2 * grp * SUB_ROWS, wp)


def _softmax_params(q_gain_scaled, k_gain, sink2, n_heads):
    bound = HEAD_DIM * jnp.max(jnp.abs(q_gain_scaled)) * jnp.max(jnp.abs(k_gain))
    fast = bound <= FAST_SOFTMAX_MAX_BOUND
    sink2 = jnp.zeros((n_heads,), F32) if sink2 is None else sink2
    stab = jnp.where(fast, jnp.maximum(bound, sink2), 0.0)
    hp = jnp.stack([stab, sink2 - stab, jnp.exp2(sink2 - stab)])
    return fast.astype(jnp.int32).reshape(1), hp


def _attn_call(q, k, v, slopes, fast, hp, *, radius, stride, grp, has_sink, want_lse):
    n, length, _ = q.shape
    cols = k.shape[2] // LANES
    heads_by_col = has_sink
    bq = next(c for c in (ATTN_STEP_ROWS, 512, 256, SUB_ROWS) if length % c == 0)
    ncol = 1 if heads_by_col else math.gcd(cols, max(1, ATTN_STEP_ROWS // bq))
    assert bq % radius == 0
    nsub = bq // SUB_ROWS
    wp = SUB_ROWS + 2 * radius
    g_rows = 2 * grp * SUB_ROWS
    nblk = length // bq
    halo_per_blk = bq // radius
    n_halo = length // radius
    tables = _bias_tables(radius, stride, slopes, hp[0].reshape(slopes.shape), grp)

    def prev_map(c, b, i):
        return (b, jnp.maximum(i * halo_per_blk - 1, 0), c)

    def next_map(c, b, i):
        return (b, jnp.minimum((i + 1) * halo_per_blk, n_halo - 1), c)

    cur_map = lambda c, b, i: (b, i, c)

    def tab_map(first, last):
        def f(c, b, i):
            code = jnp.int32(0)
            if first:
                code = code + (i == 0).astype(jnp.int32)
            if last:
                code = code + 2 * (i == nblk - 1).astype(jnp.int32)
            return (code, c if heads_by_col else 0, 0, 0)
        return f

    tab_maps = [tab_map(True, True)] if nsub == 1 else (
        [tab_map(True, False)] + [tab_map(False, False)] * (nsub > 2) + [tab_map(False, True)])

    smem = pl.BlockSpec(memory_space=pltpu.SMEM)
    in_specs = [smem, smem, pl.BlockSpec((1, bq, ncol * grp * LANES), cur_map)]
    args = [fast, hp, q]
    for arr in (k, v):
        in_specs += [pl.BlockSpec((1, radius, ncol * LANES), prev_map),
                     pl.BlockSpec((1, bq, ncol * LANES), cur_map),
                     pl.BlockSpec((1, radius, ncol * LANES), next_map)]
        args += [arr, arr, arr]
    for index_map in tab_maps:
        in_specs.append(pl.BlockSpec((None, None, g_rows, wp), index_map))
        args.append(tables)

    out_specs = [pl.BlockSpec((1, bq, ncol * grp * LANES), cur_map)]
    out_shape = [jax.ShapeDtypeStruct(q.shape, BF16)]
    if want_lse:
        out_specs.append(pl.BlockSpec((1, bq, ncol * LANES), cur_map))
        out_shape.append(jax.ShapeDtypeStruct((n, length, cols * LANES), F32))

    return pl.pallas_call(
        functools.partial(_attn_kernel, radius=radius, bq=bq, grp=grp, ncol=ncol, has_sink=has_sink,
                          has_lse=want_lse, heads_by_col=heads_by_col),
        grid=(cols // ncol, n, nblk),
        in_specs=in_specs,
        out_specs=out_specs,
        out_shape=out_shape,
        compiler_params=pltpu.CompilerParams(
            dimension_semantics=("arbitrary", "arbitrary", "arbitrary"),
            vmem_limit_bytes=VMEM_LIMIT),
        name="attn",
    )(*args)


def _ffn_chunks():
    chunks, c = [], 0
    while c < FFN_HIDDEN:
        w = min(MXU_DIM, FFN_HIDDEN - c)
        chunks.append((c, w))
        c += w
    return chunks


def _gather_residues(src_ref, stage_ref, dil):
    if dil == 1:
        return src_ref[...].astype(F32)
    slabs, rows, _ = stage_ref.shape
    w = slabs * LANES
    for r in range(dil):
        for j in range(slabs):
            piece = src_ref[:, r * w + j * LANES:r * w + (j + 1) * LANES]
            stage_ref[j, pl.ds(r, rows // dil, stride=dil), :] = piece.astype(F32)
    parts = [stage_ref[j] for j in range(slabs)]
    return parts[0] if slabs == 1 else jnp.concatenate(parts, axis=1)


def _post_kernel(*refs, dils):
    n_groups = max(len(dils), 1)
    it = iter(refs)
    x_ref = next(it)
    o_refs = [next(it) for _ in range(n_groups)]
    lse_refs = [next(it) for _ in dils]
    expand_ref = next(it) if dils else None
    p_ref = next(it)
    wo_ref, nf_ref, wg_ref, wu_ref, wd_ref, np_ref, pg_ref, pp_ref = (next(it) for _ in range(8))
    out_ref = next(it)
    o_stage = {g: next(it) for g, d in enumerate(dils) if d > 1}
    l_stage = {g: next(it) for g, d in enumerate(dils) if d > 1}

    if not dils:
        o = o_refs[0][...]
    else:
        os_ = [_gather_residues(o_refs[g], o_stage.get(g), d) for g, d in enumerate(dils)]
        lses = [_gather_residues(lse_refs[g], l_stage.get(g), d) for g, d in enumerate(dils)]
        mx = functools.reduce(jnp.maximum, lses)
        es = [jnp.exp(l - mx) for l in lses]
        inv = 1.0 / functools.reduce(lambda a, b: a + b, es)
        expand = expand_ref[...]
        pieces = []
        for g in range(n_groups):
            alpha = (es[g] * inv).astype(BF16)
            spread = jnp.dot(alpha, expand, preferred_element_type=F32)
            pieces.append((os_[g] * spread).astype(BF16))
        o = jnp.concatenate(pieces, axis=1)

    x1 = x_ref[...] + jnp.dot(o, wo_ref[...], preferred_element_type=F32)
    h = _rmsnorm_rows(x1, nf_ref[...]).astype(BF16)
    acc = None
    for c, w in _ffn_chunks():
        g = jnp.dot(h, wg_ref[:, c:c + w], preferred_element_type=F32)
        u = jnp.dot(h, wu_ref[:, c:c + w], preferred_element_type=F32)
        act = (g * jax.nn.sigmoid(g) * u).astype(BF16)
        d = jnp.dot(act, wd_ref[c:c + w, :], preferred_element_type=F32)
        acc = d if acc is None else acc + d
    x2 = x1 + acc
    hp = _rmsnorm_rows(x2, np_ref[...]).astype(BF16)
    gate = jax.nn.sigmoid(jnp.dot(hp, pg_ref[...], preferred_element_type=F32))
    proj = jnp.dot(p_ref[...].astype(BF16), pp_ref[...], preferred_element_type=F32)
    out_ref[...] = x2 + gate * proj


def _head_expand_matrix():
    c = np.arange(B_GRP * LANES)
    head = ((c % LANES) // HEAD_DIM) * B_GRP + c // LANES
    return jnp.asarray((np.arange(LANES)[:, None] == head[None, :]).astype(np.float32), dtype=BF16)


def _post_call(x2d, os_, lses, dils, p3d, layer, wo, nf, wg, wu, wd, npl, pg, pp, *, tm):
    t = x2d.shape[0]
    row = lambda r: (r, 0)
    group_dils = dils if dils else (1,)
    in_specs = [pl.BlockSpec((tm, D_MODEL), row)]
    in_specs += [pl.BlockSpec((tm // d, o.shape[1]), row) for o, d in zip(os_, group_dils)]
    in_specs += [pl.BlockSpec((tm // d, l.shape[1]), row) for l, d in zip(lses, dils)]
    extra = []
    if dils:
        extra.append(_head_expand_matrix())
        in_specs.append(_const_spec(extra[0].shape))
    in_specs.append(pl.BlockSpec((None, tm, PLE_DIM), lambda r: (layer, r, 0)))
    stacks = [nf, wg, wu, wd, npl, pg, pp]
    in_specs.append(_const_spec(wo.shape))
    in_specs += [_layer_spec(w.shape, layer) for w in stacks]
    scratch = [pltpu.VMEM((B_GRP, tm, LANES), F32) for d in dils if d > 1]
    scratch += [pltpu.VMEM((1, tm, LANES), F32) for d in dils if d > 1]
    return pl.pallas_call(
        functools.partial(_post_kernel, dils=tuple(dils)),
        grid=(t // tm,),
        in_specs=in_specs,
        out_specs=pl.BlockSpec((tm, D_MODEL), row),
        out_shape=jax.ShapeDtypeStruct((t, D_MODEL), F32),
        scratch_shapes=scratch,
        compiler_params=pltpu.CompilerParams(
            dimension_semantics=("arbitrary",), vmem_limit_bytes=VMEM_LIMIT),
        name="post",
    )(x2d, *os_, *lses, *extra, p3d, wo, *stacks)


def _alibi_slopes(n):
    return 2.0 ** (-8.0 * np.arange(1, n + 1, dtype=np.float64) / n)


def _a_head_order():
    return [(2 * s + par) * A_GRP + a
            for s in range(A_KV_HEADS // 2) for a in range(A_GRP) for par in range(2)]


def _b_head_order():
    return [g * B_Q_PER_GROUP + par * B_GRP + a
            for g in range(len(B_GROUPS)) for a in range(B_GRP) for par in range(2)]


def _slab_param_order(n_slabs, grp):
    return [(s * grp + a) * 2 + par for s in range(n_slabs) for par in range(2) for a in range(grp)]


def _q_gain_scaled(q_gain):
    return q_gain * (HEAD_DIM ** -0.5 * LOG2E)


def _prep_qkv_weights(wqkv, wo, q_gain, k_gain, hq, hkv, order):
    cq, ck = hq * HEAD_DIM, hkv * HEAD_DIM
    w16, wo16 = wqkv.astype(BF16), wo.astype(BF16)
    head = lambda h: slice(h * HEAD_DIM, (h + 1) * HEAD_DIM)
    w = jnp.concatenate([w16[:, head(h)] for h in order] + [w16[:, cq:]], axis=1)
    wo_p = jnp.concatenate([wo16[head(h)] for h in order], axis=0)
    root = HEAD_DIM ** 0.5
    gq = jnp.tile(_q_gain_scaled(q_gain) * root, hq).reshape(1, cq)
    gk = jnp.tile(k_gain * root, hkv).reshape(1, ck)
    return w, wo_p, gq, gk


def _trunk(x, p, norm_mix, a_prep, b_prep, ffn_prep, *, tm):
    b, s, _ = x.shape
    t = b * s
    x2d = x.reshape(t, D_MODEL)
    p3d = p.reshape(p.shape[0], t, PLE_DIM)

    w, wo_p, gq, gk, slopes, fast, hp = a_prep
    cq, ck = A_Q_HEADS * HEAD_DIM, A_KV_HEADS * HEAD_DIM
    q, k, v = _qkv_call(x2d, norm_mix[0].reshape(1, -1), w, gq, gk, cq=cq, ck=ck,
                        outs=[(0, cq, 1), (cq, ck, 1), (cq + ck, ck, 1)], tm=2 * tm)
    o = _attn_call(q.reshape(b, s, cq), k.reshape(b, s, ck), v.reshape(b, s, ck), slopes, fast, hp,
                   radius=A_RADIUS, stride=1, grp=A_GRP, has_sink=True, want_lse=False)[0]
    x2d = _post_call(x2d, [o.reshape(t, cq)], [], (), p3d, 0, wo_p, *ffn_prep, tm=tm)

    w, wo_p, gq, gk, slopes, fast, hp = b_prep
    cq, ck = B_Q_HEADS * HEAD_DIM, B_KV_HEADS * HEAD_DIM
    gq_w = B_GRP * LANES
    dils = tuple(d for _, d in B_GROUPS)
    outs = []
    for g, d in enumerate(dils):
        outs += [(g * gq_w, gq_w, d), (cq + g * LANES, LANES, d), (cq + ck + g * LANES, LANES, d)]
    qkv = _qkv_call(x2d, norm_mix[1].reshape(1, -1), w, gq, gk, cq=cq, ck=ck, outs=outs, tm=2 * tm)
    os_, lses = [], []
    for g, (window, d) in enumerate(B_GROUPS):
        radius = window // (2 * d)
        qg, kg, vg = (a.reshape(b, s // d, a.shape[1]) for a in qkv[3 * g:3 * g + 3])
        og, lg = _attn_call(qg, kg, vg, slopes[g:g + 1], fast, hp[:, g * B_Q_PER_GROUP:(g + 1) * B_Q_PER_GROUP],
                            radius=radius, stride=d, grp=B_GRP, has_sink=False, want_lse=True)
        os_.append(og.reshape(t // d, d * gq_w))
        lses.append(lg.reshape(t // d, d * LANES))
    x2d = _post_call(x2d, os_, lses, dils, p3d, 1, wo_p, *ffn_prep, tm=tm)
    return x2d.reshape(b, s, D_MODEL)


def kernel(x_prompt, x_sample, p_prompt, p_sample, norm_mix, norm_ffn, norm_ple,
           a_wqkv, a_wo, a_q_gain, a_k_gain, a_sink,
           b_wqkv, b_wo, b_q_gain, b_k_gain,
           ffn_w_gate, ffn_w_up, ffn_w_down, ple_w_gate, ple_w_proj):
    a_order, b_order = _a_head_order(), _b_head_order()
    a_slabs = A_KV_HEADS // 2

    a_par = np.asarray(a_order)[_slab_param_order(a_slabs, A_GRP)]
    a_slopes = jnp.asarray(_alibi_slopes(A_Q_HEADS)[a_par].reshape(a_slabs, 2 * A_GRP), dtype=F32)
    a_prep = _prep_qkv_weights(a_wqkv[0], a_wo[0], a_q_gain[0], a_k_gain[0], A_Q_HEADS, A_KV_HEADS, a_order)
    a_prep += (a_slopes,) + _softmax_params(_q_gain_scaled(a_q_gain[0]), a_k_gain[0], a_sink[0][a_par] * LOG2E,
                                            A_Q_HEADS)

    b_par = np.asarray(b_order)[_slab_param_order(len(B_GROUPS), B_GRP)]
    b_slopes = jnp.asarray(_alibi_slopes(B_Q_HEADS)[b_par].reshape(len(B_GROUPS), 2 * B_GRP), dtype=F32)
    b_prep = _prep_qkv_weights(b_wqkv[0], b_wo[0], b_q_gain[0], b_k_gain[0], B_Q_HEADS, B_KV_HEADS, b_order)
    b_prep += (b_slopes,) + _softmax_params(_q_gain_scaled(b_q_gain[0]), b_k_gain[0], None, B_Q_HEADS)

    ffn_prep = (norm_ffn[:, None, :], ffn_w_gate.astype(BF16), ffn_w_up.astype(BF16), ffn_w_down.astype(BF16),
                norm_ple[:, None, :], ple_w_gate.astype(BF16), ple_w_proj.astype(BF16))

    run = functools.partial(_trunk, norm_mix=norm_mix, a_prep=a_prep, b_prep=b_prep,
                            ffn_prep=ffn_prep, tm=512)
    return (run(x_prompt, p_prompt), run(x_sample, p_sample))
```

```python
import functools
import math

import numpy as np
import jax
import jax.numpy as jnp
from jax import lax
from jax.experimental import pallas as pl
from jax.experimental.pallas import tpu as pltpu

D_MODEL = 1024
HEAD_DIM = 64
PLE_DIM = 256
FFN_HIDDEN = 2816
EPS = 1e-6

A_Q_HEADS = 16
A_KV_HEADS = 4
A_RADIUS = 128
B_GROUPS = ((128, 1), (512, 4), (2048, 16))
B_Q_PER_GROUP = 6
B_KV_PER_GROUP = 2
B_Q_HEADS = B_Q_PER_GROUP * len(B_GROUPS)
B_KV_HEADS = B_KV_PER_GROUP * len(B_GROUPS)
A_GRP = A_Q_HEADS // A_KV_HEADS
B_GRP = B_Q_PER_GROUP // B_KV_PER_GROUP

LANES = 128
MXU_DIM = 256
SUB_ROWS = 128
ATTN_STEP_ROWS = 2048
MASK_BIAS = -1e32
VMEM_LIMIT = 56 * 1024 * 1024
LOG2E = math.log2(math.e)
LN2 = math.log(2.0)
FAST_SOFTMAX_MAX_BOUND = 30.0 * LOG2E

F32 = jnp.float32
BF16 = jnp.bfloat16


def _const_spec(shape):
    nd = len(shape)
    return pl.BlockSpec(shape, lambda *_: (0,) * nd, pipeline_mode=pl.Buffered(1))


def _layer_spec(shape, layer):
    return pl.BlockSpec((None,) + tuple(shape[1:]), lambda *_: (layer, 0, 0), pipeline_mode=pl.Buffered(1))


def _rmsnorm_rows(x, gain):
    ms = jnp.mean(x * x, axis=-1, keepdims=True)
    return x * lax.rsqrt(ms + EPS) * gain


def _head_sumsq(t):
    width = t.shape[1]
    lane = lax.broadcasted_iota(jnp.int32, (t.shape[0], LANES), 1)
    parts = []
    for c in range(0, width, LANES):
        blk = t[:, c:c + LANES]
        sq = blk * blk
        tot = jnp.sum(sq, axis=-1, keepdims=True)
        lo = jnp.sum(jnp.where(lane < HEAD_DIM, sq, 0.0), axis=-1, keepdims=True)
        parts.append(jnp.where(lane < HEAD_DIM, lo, tot - lo))
    return parts[0] if len(parts) == 1 else jnp.concatenate(parts, axis=1)


def _qkv_kernel(x_ref, gn_ref, w_ref, gq_ref, gk_ref, *rest, cq, ck, outs):
    out_refs = rest[:len(outs)]
    stage_ref = rest[len(outs)] if len(rest) > len(outs) else None
    tm = x_ref.shape[0]
    h = _rmsnorm_rows(x_ref[...], gn_ref[...]).astype(BF16)
    qkv = jnp.dot(h, w_ref[...], preferred_element_type=F32)
    q = qkv[:, :cq]
    k = qkv[:, cq:cq + ck]
    q = q * lax.rsqrt(_head_sumsq(q) + HEAD_DIM * EPS) * gq_ref[...]
    k = k * lax.rsqrt(_head_sumsq(k) + HEAD_DIM * EPS) * gk_ref[...]
    full = jnp.concatenate([q, k, qkv[:, cq + ck:]], axis=1)
    if stage_ref is not None:
        for j in range(stage_ref.shape[0]):
            stage_ref[j] = full[:, j * LANES:(j + 1) * LANES]
    for (off, w, dil), o_ref in zip(outs, out_refs):
        if dil == 1:
            o_ref[...] = full[:, off:off + w].astype(BF16)
        else:
            rows = tm // dil
            for r in range(dil):
                for j in range(w // LANES):
                    piece = stage_ref[off // LANES + j, pl.ds(r, rows, stride=dil), :]
                    o_ref[:, r * w + j * LANES:r * w + (j + 1) * LANES] = piece.astype(BF16)


def _qkv_call(x2d, gn, w, gq, gk, *, cq, ck, outs, tm):
    t = x2d.shape[0]
    c = cq + 2 * ck
    row = lambda r: (r, 0)
    staged = any(d > 1 for _, _, d in outs)
    return pl.pallas_call(
        functools.partial(_qkv_kernel, cq=cq, ck=ck, outs=tuple(outs)),
        grid=(t // tm,),
        in_specs=[
            pl.BlockSpec((tm, D_MODEL), row),
            _const_spec((1, D_MODEL)),
            _const_spec((D_MODEL, c)),
            _const_spec((1, cq)),
            _const_spec((1, ck)),
        ],
        out_specs=[pl.BlockSpec((tm // d, d * w_), row) for _, w_, d in outs],
        out_shape=[jax.ShapeDtypeStruct((t // d, d * w_), BF16) for _, w_, d in outs],
        scratch_shapes=[pltpu.VMEM((c // LANES, tm, LANES), F32)] if staged else [],
        compiler_params=pltpu.CompilerParams(
            dimension_semantics=("arbitrary",), vmem_limit_bytes=VMEM_LIMIT),
        name="qkv",
    )(x2d, gn, w, gq, gk)


def _attn_kernel(*refs, radius, bq, grp, ncol, has_sink, has_lse, heads_by_col):
    nsub = bq // SUB_ROWS
    wp = SUB_ROWS + 2 * radius
    it = iter(refs)
    fast_ref = next(it)
    hp_ref = next(it)
    q_ref = next(it)
    kp_ref, kc_ref, kn_ref, vp_ref, vc_ref, vn_ref = (next(it) for _ in range(6))
    tab_refs = [next(it) for _ in range(min(nsub, 3))]
    o_ref = next(it)
    lse_ref = next(it) if has_lse else None

    head0 = pl.program_id(0) * (2 * grp) if heads_by_col else 0

    def table(refs, t):
        if t == 0:
            return refs[0]
        return refs[-1] if t == nsub - 1 else refs[1]

    def body(fast):
        for col in range(ncol):
            kcols = slice(col * LANES, (col + 1) * LANES)
            kwin = jnp.concatenate([kp_ref[0, :, kcols], kc_ref[0, :, kcols], kn_ref[0, :, kcols]], axis=0)
            vwin = jnp.concatenate([vp_ref[0, :, kcols], vc_ref[0, :, kcols], vn_ref[0, :, kcols]], axis=0)
            klane = lax.broadcasted_iota(jnp.int32, kwin.shape, 1)
            kzero = jnp.zeros_like(kwin)
            k_half = [jnp.where(klane < HEAD_DIM, kwin, kzero),
                      jnp.where(klane >= HEAD_DIM, kwin, kzero)]
            (bound_softmax_column if fast else rowmax_softmax_column)(col, k_half, vwin)

    def stacked_q(col, t):
        q0 = col * grp * LANES
        return jnp.concatenate(
            [q_ref[0, t * SUB_ROWS:(t + 1) * SUB_ROWS, q0 + a * LANES:q0 + (a + 1) * LANES]
             for a in range(grp)], axis=0)

    def bound_softmax_column(col, k_half, vwin):
        kcols = slice(col * LANES, (col + 1) * LANES)
        q0 = col * grp * LANES
        g_rows = grp * SUB_ROWS
        olane = lax.broadcasted_iota(jnp.int32, (SUB_ROWS, LANES), 1)
        ones = jnp.ones((wp, LANES), BF16)
        for t in range(nsub):
            r0 = t * SUB_ROWS
            qst = stacked_q(col, t)
            probs = []
            for par in range(2):
                sc = lax.dot_general(qst, k_half[par][r0:r0 + wp], (((1,), (1,)), ((), ())),
                                     preferred_element_type=F32)
                probs.append(jnp.exp2(sc + table(tab_refs, t)[par * g_rows:(par + 1) * g_rows, :]).astype(BF16))
            pv = jnp.dot(jnp.concatenate(probs, axis=0), jnp.concatenate([vwin[r0:r0 + wp], ones], axis=1),
                         preferred_element_type=F32)
            lse_acc = jnp.zeros((SUB_ROWS, LANES), F32) if has_lse else None
            for a in range(grp):
                halves = []
                for par in range(2):
                    h = par * grp + a
                    rows = slice(h * SUB_ROWS, (h + 1) * SUB_ROWS)
                    den = pv[rows, LANES:]
                    if has_sink:
                        den = den + hp_ref[2, head0 + h]
                    halves.append(pv[rows, :LANES] / den)
                    if has_lse:
                        lse_acc = jnp.where(olane == h, LN2 * hp_ref[0, head0 + h] + jnp.log(den), lse_acc)
                o_ref[0, r0:r0 + SUB_ROWS, q0 + a * LANES:q0 + (a + 1) * LANES] = jnp.where(
                    olane < HEAD_DIM, halves[0], halves[1]).astype(BF16)
            if has_lse:
                lse_ref[0, r0:r0 + SUB_ROWS, kcols] = lse_acc

    def rowmax_softmax_column(col, k_half, vwin):
        kcols = slice(col * LANES, (col + 1) * LANES)
        q0 = col * grp * LANES
        olane = lax.broadcasted_iota(jnp.int32, (SUB_ROWS, LANES), 1)
        for t in range(nsub):
            r0 = t * SUB_ROWS
            qst = stacked_q(col, t)
            probs, dens, maxes = [], [], []
            for par in range(2):
                sc = lax.dot_general(qst, k_half[par][r0:r0 + wp], (((1,), (1,)), ((), ())),
                                     preferred_element_type=F32)
                for a in range(grp):
                    h = par * grp + a
                    sa = sc[a * SUB_ROWS:(a + 1) * SUB_ROWS] + table(tab_refs, t)[h * SUB_ROWS:(h + 1) * SUB_ROWS, :]
                    m = jnp.max(sa, axis=-1, keepdims=True)
                    if has_sink:
                        sink = hp_ref[1, head0 + h]
                        m = jnp.maximum(m, sink)
                    p = jnp.exp2(sa - m)
                    den = jnp.sum(p, axis=-1, keepdims=True)
                    if has_sink:
                        den = den + jnp.exp2(sink - m)
                    probs.append(p.astype(BF16))
                    dens.append(den)
                    maxes.append(m)
            pv = jnp.dot(jnp.concatenate(probs, axis=0), vwin[r0:r0 + wp], preferred_element_type=F32)
            lse_acc = jnp.zeros((SUB_ROWS, LANES), F32) if has_lse else None
            for a in range(grp):
                halves = []
                for par in range(2):
                    h = par * grp + a
                    halves.append(pv[h * SUB_ROWS:(h + 1) * SUB_ROWS] / dens[h])
                    if has_lse:
                        lse_acc = jnp.where(olane == h, LN2 * maxes[h] + jnp.log(dens[h]), lse_acc)
                o_ref[0, r0:r0 + SUB_ROWS, q0 + a * LANES:q0 + (a + 1) * LANES] = jnp.where(
                    olane < HEAD_DIM, halves[0], halves[1]).astype(BF16)
            if has_lse:
                lse_ref[0, r0:r0 + SUB_ROWS, kcols] = lse_acc

    @pl.when(fast_ref[0] != 0)
    def _():
        body(True)

    @pl.when(fast_ref[0] == 0)
    def _():
        body(False)


def _bias_tables(radius, stride, slopes, stab, grp):
    wp = SUB_ROWS + 2 * radius
    i = np.arange(SUB_ROWS)[:, None]
    j = np.arange(wp)[None, :]
    rel = j - radius - i
    band = np.abs(rel) <= radius
    tabs = []
    for code in range(4):
        valid = band
        if code & 1:
            valid = valid & (j >= radius)
        if code & 2:
            valid = valid & (j < SUB_ROWS + radius)
        tabs.append(np.where(valid, -float(stride) * LOG2E * np.abs(rel), MASK_BIAS))
    base = jnp.asarray(np.stack(tabs), dtype=F32)
    slabs = slopes.shape[0]
    full = base[:, None, None] * slopes[None, :, :, None, None] - stab[None, :, :, None, None]
    return full.reshape(4, slabs, 2 * grp * SUB_ROWS, wp)


def _softmax_params(q_gain_scaled, k_gain, sink2, n_heads):
    bound = HEAD_DIM * jnp.max(jnp.abs(q_gain_scaled)) * jnp.max(jnp.abs(k_gain))
    fast = bound <= FAST_SOFTMAX_MAX_BOUND
    sink2 = jnp.zeros((n_heads,), F32) if sink2 is None else sink2
    stab = jnp.where(fast, jnp.maximum(bound, sink2), 0.0)
    hp = jnp.stack([stab, sink2 - stab, jnp.exp2(sink2 - stab)])
    return fast.astype(jnp.int32).reshape(1), hp


def _attn_call(q, k, v, slopes, fast, hp, *, radius, stride, grp, has_sink, want_lse):
    n, length, _ = q.shape
    cols = k.shape[2] // LANES
    heads_by_col = has_sink
    bq = next(c for c in (ATTN_STEP_ROWS, 512, 256, SUB_ROWS) if length % c == 0)
    ncol = 1 if heads_by_col else math.gcd(cols, max(1, ATTN_STEP_ROWS // bq))
    assert bq % radius == 0
    nsub = bq // SUB_ROWS
    wp = SUB_ROWS + 2 * radius
    g_rows = 2 * grp * SUB_ROWS
    nblk = length // bq
    halo_per_blk = bq // radius
    n_halo = length // radius
    tables = _bias_tables(radius, stride, slopes, hp[0].reshape(slopes.shape), grp)

    def prev_map(c, b, i):
        return (b, jnp.maximum(i * halo_per_blk - 1, 0), c)

    def next_map(c, b, i):
        return (b, jnp.minimum((i + 1) * halo_per_blk, n_halo - 1), c)

    cur_map = lambda c, b, i: (b, i, c)

    def tab_map(first, last):
        def f(c, b, i):
            code = jnp.int32(0)
            if first:
                code = code + (i == 0).astype(jnp.int32)
            if last:
                code = code + 2 * (i == nblk - 1).astype(jnp.int32)
            return (code, c if heads_by_col else 0, 0, 0)
        return f

    tab_maps = [tab_map(True, True)] if nsub == 1 else (
        [tab_map(True, False)] + [tab_map(False, False)] * (nsub > 2) + [tab_map(False, True)])

    smem = pl.BlockSpec(memory_space=pltpu.SMEM)
    in_specs = [smem, smem, pl.BlockSpec((1, bq, ncol * grp * LANES), cur_map)]
    args = [fast, hp, q]
    for arr in (k, v):
        in_specs += [pl.BlockSpec((1, radius, ncol * LANES), prev_map),
                     pl.BlockSpec((1, bq, ncol * LANES), cur_map),
                     pl.BlockSpec((1, radius, ncol * LANES), next_map)]
        args += [arr, arr, arr]
    for index_map in tab_maps:
        in_specs.append(pl.BlockSpec((None, None, g_rows, wp), index_map))
        args.append(tables)

    out_specs = [pl.BlockSpec((1, bq, ncol * grp * LANES), cur_map)]
    out_shape = [jax.ShapeDtypeStruct(q.shape, BF16)]
    if want_lse:
        out_specs.append(pl.BlockSpec((1, bq, ncol * LANES), cur_map))
        out_shape.append(jax.ShapeDtypeStruct((n, length, cols * LANES), F32))

    return pl.pallas_call(
        functools.partial(_attn_kernel, radius=radius, bq=bq, grp=grp, ncol=ncol, has_sink=has_sink,
                          has_lse=want_lse, heads_by_col=heads_by_col),
        grid=(cols // ncol, n, nblk),
        in_specs=in_specs,
        out_specs=out_specs,
        out_shape=out_shape,
        compiler_params=pltpu.CompilerParams(
            dimension_semantics=("arbitrary", "arbitrary", "arbitrary"),
            vmem_limit_bytes=VMEM_LIMIT),
        name="attn",
    )(*args)


def _ffn_chunks():
    chunks, c = [], 0
    while c < FFN_HIDDEN:
        w = min(MXU_DIM, FFN_HIDDEN - c)
        chunks.append((c, w))
        c += w
    return chunks


def _gather_residues(src_ref, stage_ref, dil):
    if dil == 1:
        return src_ref[...].astype(F32)
    slabs, rows, _ = stage_ref.shape
    w = slabs * LANES
    for r in range(dil):
        for j in range(slabs):
            piece = src_ref[:, r * w + j * LANES:r * w + (j + 1) * LANES]
            stage_ref[j, pl.ds(r, rows // dil, stride=dil), :] = piece.astype(F32)
    parts = [stage_ref[j] for j in range(slabs)]
    return parts[0] if slabs == 1 else jnp.concatenate(parts, axis=1)


def _post_kernel(*refs, dils):
    n_groups = max(len(dils), 1)
    it = iter(refs)
    x_ref = next(it)
    o_refs = [next(it) for _ in range(n_groups)]
    lse_refs = [next(it) for _ in dils]
    expand_ref = next(it) if dils else None
    p_ref = next(it)
    wo_ref, nf_ref, wg_ref, wu_ref, wd_ref, np_ref, pg_ref, pp_ref = (next(it) for _ in range(8))
    out_ref = next(it)
    o_stage = {g: next(it) for g, d in enumerate(dils) if d > 1}
    l_stage = {g: next(it) for g, d in enumerate(dils) if d > 1}
    act_ref = next(it)

    if not dils:
        o = o_refs[0][...]
    else:
        os_ = [_gather_residues(o_refs[g], o_stage.get(g), d) for g, d in enumerate(dils)]
        lses = [_gather_residues(lse_refs[g], l_stage.get(g), d) for g, d in enumerate(dils)]
        mx = functools.reduce(jnp.maximum, lses)
        es = [jnp.exp(l - mx) for l in lses]
        inv = 1.0 / functools.reduce(lambda a, b: a + b, es)
        expand = expand_ref[...]
        pieces = []
        for g in range(n_groups):
            alpha = (es[g] * inv).astype(BF16)
            spread = jnp.dot(alpha, expand, preferred_element_type=F32)
            pieces.append((os_[g] * spread).astype(BF16))
        o = jnp.concatenate(pieces, axis=1)

    x1 = x_ref[...] + jnp.dot(o, wo_ref[...], preferred_element_type=F32)
    h = _rmsnorm_rows(x1, nf_ref[...]).astype(BF16)
    for c, w in _ffn_chunks():
        g = jnp.dot(h, wg_ref[:, c:c + w], preferred_element_type=F32)
        u = jnp.dot(h, wu_ref[:, c:c + w], preferred_element_type=F32)
        act_ref[:, c:c + w] = (g * jax.nn.sigmoid(g) * u).astype(BF16)
    x2 = x1 + jnp.dot(act_ref[...], wd_ref[...], preferred_element_type=F32)
    hp = _rmsnorm_rows(x2, np_ref[...]).astype(BF16)
    gate = jax.nn.sigmoid(jnp.dot(hp, pg_ref[...], preferred_element_type=F32))
    proj = jnp.dot(p_ref[...].astype(BF16), pp_ref[...], preferred_element_type=F32)
    out_ref[...] = x2 + gate * proj


def _head_expand_matrix():
    c = np.arange(B_GRP * LANES)
    head = ((c % LANES) // HEAD_DIM) * B_GRP + c // LANES
    return jnp.asarray((np.arange(LANES)[:, None] == head[None, :]).astype(np.float32), dtype=BF16)


def _post_call(x2d, os_, lses, dils, p3d, layer, wo, nf, wg, wu, wd, npl, pg, pp, *, tm):
    t = x2d.shape[0]
    row = lambda r: (r, 0)
    group_dils = dils if dils else (1,)
    in_specs = [pl.BlockSpec((tm, D_MODEL), row)]
    in_specs += [pl.BlockSpec((tm // d, o.shape[1]), row) for o, d in zip(os_, group_dils)]
    in_specs += [pl.BlockSpec((tm // d, l.shape[1]), row) for l, d in zip(lses, dils)]
    extra = []
    if dils:
        extra.append(_head_expand_matrix())
        in_specs.append(_const_spec(extra[0].shape))
    in_specs.append(pl.BlockSpec((None, tm, PLE_DIM), lambda r: (layer, r, 0)))
    stacks = [nf, wg, wu, wd, npl, pg, pp]
    in_specs.append(_const_spec(wo.shape))
    in_specs += [_layer_spec(w.shape, layer) for w in stacks]
    scratch = [pltpu.VMEM((B_GRP, tm, LANES), F32) for d in dils if d > 1]
    scratch += [pltpu.VMEM((1, tm, LANES), F32) for d in dils if d > 1]
    scratch.append(pltpu.VMEM((tm, FFN_HIDDEN), BF16))
    return pl.pallas_call(
        functools.partial(_post_kernel, dils=tuple(dils)),
        grid=(t // tm,),
        in_specs=in_specs,
        out_specs=pl.BlockSpec((tm, D_MODEL), row),
        out_shape=jax.ShapeDtypeStruct((t, D_MODEL), F32),
        scratch_shapes=scratch,
        compiler_params=pltpu.CompilerParams(
            dimension_semantics=("arbitrary",), vmem_limit_bytes=VMEM_LIMIT),
        name="post",
    )(x2d, *os_, *lses, *extra, p3d, wo, *stacks)


def _alibi_slopes(n):
    return 2.0 ** (-8.0 * np.arange(1, n + 1, dtype=np.float64) / n)


def _a_head_order():
    return [(2 * s + par) * A_GRP + a
            for s in range(A_KV_HEADS // 2) for a in range(A_GRP) for par in range(2)]


def _b_head_order():
    return [g * B_Q_PER_GROUP + par * B_GRP + a
            for g in range(len(B_GROUPS)) for a in range(B_GRP) for par in range(2)]


def _slab_param_order(n_slabs, grp):
    return [(s * grp + a) * 2 + par for s in range(n_slabs) for par in range(2) for a in range(grp)]


def _q_gain_scaled(q_gain):
    return q_gain * (HEAD_DIM ** -0.5 * LOG2E)


def _prep_qkv_weights(wqkv, wo, q_gain, k_gain, hq, hkv, order):
    cq, ck = hq * HEAD_DIM, hkv * HEAD_DIM
    w16, wo16 = wqkv.astype(BF16), wo.astype(BF16)
    head = lambda h: slice(h * HEAD_DIM, (h + 1) * HEAD_DIM)
    w = jnp.concatenate([w16[:, head(h)] for h in order] + [w16[:, cq:]], axis=1)
    wo_p = jnp.concatenate([wo16[head(h)] for h in order], axis=0)
    root = HEAD_DIM ** 0.5
    gq = jnp.tile(_q_gain_scaled(q_gain) * root, hq).reshape(1, cq)
    gk = jnp.tile(k_gain * root, hkv).reshape(1, ck)
    return w, wo_p, gq, gk


def _trunk(x, p, norm_mix, a_prep, b_prep, ffn_prep, *, tm):
    b, s, _ = x.shape
    t = b * s
    x2d = x.reshape(t, D_MODEL)
    p3d = p.reshape(p.shape[0], t, PLE_DIM)

    w, wo_p, gq, gk, slopes, fast, hp = a_prep
    cq, ck = A_Q_HEADS * HEAD_DIM, A_KV_HEADS * HEAD_DIM
    q, k, v = _qkv_call(x2d, norm_mix[0].reshape(1, -1), w, gq, gk, cq=cq, ck=ck,
                        outs=[(0, cq, 1), (cq, ck, 1), (cq + ck, ck, 1)], tm=2 * tm)
    o = _attn_call(q.reshape(b, s, cq), k.reshape(b, s, ck), v.reshape(b, s, ck), slopes, fast, hp,
                   radius=A_RADIUS, stride=1, grp=A_GRP, has_sink=True, want_lse=False)[0]
    x2d = _post_call(x2d, [o.reshape(t, cq)], [], (), p3d, 0, wo_p, *ffn_prep, tm=tm)

    w, wo_p, gq, gk, slopes, fast, hp = b_prep
    cq, ck = B_Q_HEADS * HEAD_DIM, B_KV_HEADS * HEAD_DIM
    gq_w = B_GRP * LANES
    dils = tuple(d for _, d in B_GROUPS)
    outs = []
    for g, d in enumerate(dils):
        outs += [(g * gq_w, gq_w, d), (cq + g * LANES, LANES, d), (cq + ck + g * LANES, LANES, d)]
    qkv = _qkv_call(x2d, norm_mix[1].reshape(1, -1), w, gq, gk, cq=cq, ck=ck, outs=outs, tm=2 * tm)
    os_, lses = [], []
    for g, (window, d) in enumerate(B_GROUPS):
        radius = window // (2 * d)
        qg, kg, vg = (a.reshape(b, s // d, a.shape[1]) for a in qkv[3 * g:3 * g + 3])
        og, lg = _attn_call(qg, kg, vg, slopes[g:g + 1], fast, hp[:, g * B_Q_PER_GROUP:(g + 1) * B_Q_PER_GROUP],
                            radius=radius, stride=d, grp=B_GRP, has_sink=False, want_lse=True)
        os_.append(og.reshape(t // d, d * gq_w))
        lses.append(lg.reshape(t // d, d * LANES))
    x2d = _post_call(x2d, os_, lses, dils, p3d, 1, wo_p, *ffn_prep, tm=tm)
    return x2d.reshape(b, s, D_MODEL)


def kernel(x_prompt, x_sample, p_prompt, p_sample, norm_mix, norm_ffn, norm_ple,
           a_wqkv, a_wo, a_q_gain, a_k_gain, a_sink,
           b_wqkv, b_wo, b_q_gain, b_k_gain,
           ffn_w_gate, ffn_w_up, ffn_w_down, ple_w_gate, ple_w_proj):
    a_order, b_order = _a_head_order(), _b_head_order()
    a_slabs = A_KV_HEADS // 2

    a_par = np.asarray(a_order)[_slab_param_order(a_slabs, A_GRP)]
    a_slopes = jnp.asarray(_alibi_slopes(A_Q_HEADS)[a_par].reshape(a_slabs, 2 * A_GRP), dtype=F32)
    a_prep = _prep_qkv_weights(a_wqkv[0], a_wo[0], a_q_gain[0], a_k_gain[0], A_Q_HEADS, A_KV_HEADS, a_order)
    a_prep += (a_slopes,) + _softmax_params(_q_gain_scaled(a_q_gain[0]), a_k_gain[0], a_sink[0][a_par] * LOG2E,
                                            A_Q_HEADS)

    b_par = np.asarray(b_order)[_slab_param_order(len(B_GROUPS), B_GRP)]
    b_slopes = jnp.asarray(_alibi_slopes(B_Q_HEADS)[b_par].reshape(len(B_GROUPS), 2 * B_GRP), dtype=F32)
    b_prep = _prep_qkv_weights(b_wqkv[0], b_wo[0], b_q_gain[0], b_k_gain[0], B_Q_HEADS, B_KV_HEADS, b_order)
    b_prep += (b_slopes,) + _softmax_params(_q_gain_scaled(b_q_gain[0]), b_k_gain[0], None, B_Q_HEADS)

    ffn_prep = (norm_ffn[:, None, :], ffn_w_gate.astype(BF16), ffn_w_up.astype(BF16), ffn_w_down.astype(BF16),
                norm_ple[:, None, :], ple_w_gate.astype(BF16), ple_w_proj.astype(BF16))

    run = functools.partial(_trunk, norm_mix=norm_mix, a_prep=a_prep, b_prep=b_prep,
                            ffn_prep=ffn_prep, tm=512)
    return (run(x_prompt, p_prompt), run(x_sample, p_sample))
```

```python
import functools
import math

import numpy as np
import jax
import jax.numpy as jnp
from jax import lax
from jax.experimental import pallas as pl
from jax.experimental.pallas import tpu as pltpu

D_MODEL = 1024
HEAD_DIM = 64
PLE_DIM = 256
FFN_HIDDEN = 2816
EPS = 1e-6

A_Q_HEADS = 16
A_KV_HEADS = 4
A_RADIUS = 128
B_GROUPS = ((128, 1), (512, 4), (2048, 16))
B_Q_PER_GROUP = 6
B_KV_PER_GROUP = 2
B_Q_HEADS = B_Q_PER_GROUP * len(B_GROUPS)
B_KV_HEADS = B_KV_PER_GROUP * len(B_GROUPS)
A_GRP = A_Q_HEADS // A_KV_HEADS
B_GRP = B_Q_PER_GROUP // B_KV_PER_GROUP

LANES = 128
MXU_DIM = 256
SUB_ROWS = 128
QKV_CHAIN_ROWS = 512
ATTN_STEP_ROWS = 2048
MASK_BIAS = -1e32
VMEM_LIMIT = 56 * 1024 * 1024
LOG2E = math.log2(math.e)
LN2 = math.log(2.0)
FAST_SOFTMAX_MAX_BOUND = 30.0 * LOG2E

F32 = jnp.float32
BF16 = jnp.bfloat16


def _const_spec(shape):
    nd = len(shape)
    return pl.BlockSpec(shape, lambda *_: (0,) * nd, pipeline_mode=pl.Buffered(1))


def _layer_spec(shape, layer):
    return pl.BlockSpec((None,) + tuple(shape[1:]), lambda *_: (layer, 0, 0), pipeline_mode=pl.Buffered(1))


def _rmsnorm_rows(x, gain):
    ms = jnp.mean(x * x, axis=-1, keepdims=True)
    return x * lax.rsqrt(ms + EPS) * gain


def _head_sumsq(t):
    width = t.shape[1]
    lane = lax.broadcasted_iota(jnp.int32, (t.shape[0], LANES), 1)
    parts = []
    for c in range(0, width, LANES):
        blk = t[:, c:c + LANES]
        sq = blk * blk
        tot = jnp.sum(sq, axis=-1, keepdims=True)
        lo = jnp.sum(jnp.where(lane < HEAD_DIM, sq, 0.0), axis=-1, keepdims=True)
        parts.append(jnp.where(lane < HEAD_DIM, lo, tot - lo))
    return parts[0] if len(parts) == 1 else jnp.concatenate(parts, axis=1)


def _normed_qkv(x, gn_ref, w_ref, gq_ref, gk_ref, cq, ck):
    h = _rmsnorm_rows(x, gn_ref[...]).astype(BF16)
    qkv = jnp.dot(h, w_ref[...], preferred_element_type=F32)
    q = qkv[:, :cq]
    k = qkv[:, cq:cq + ck]
    q = q * lax.rsqrt(_head_sumsq(q) + HEAD_DIM * EPS) * gq_ref[...]
    k = k * lax.rsqrt(_head_sumsq(k) + HEAD_DIM * EPS) * gk_ref[...]
    return jnp.concatenate([q, k, qkv[:, cq + ck:]], axis=1)


def _qkv_kernel(x_ref, gn_ref, w_ref, gq_ref, gk_ref, *rest, cq, ck, outs):
    out_refs = rest[:len(outs)]
    stage_ref = rest[len(outs)] if len(rest) > len(outs) else None
    tm = x_ref.shape[0]
    cr = min(tm, QKV_CHAIN_ROWS)
    for c in range(tm // cr):
        full = _normed_qkv(x_ref[c * cr:(c + 1) * cr, :], gn_ref, w_ref, gq_ref, gk_ref, cq, ck)
        if stage_ref is not None:
            for j in range(stage_ref.shape[1]):
                stage_ref[c, j] = full[:, j * LANES:(j + 1) * LANES]
        for (off, w, dil), o_ref in zip(outs, out_refs):
            rows = cr // dil
            out_rows = slice(c * rows, (c + 1) * rows)
            if dil == 1:
                o_ref[out_rows, :] = full[:, off:off + w].astype(BF16)
                continue
            for r in range(dil):
                for j in range(w // LANES):
                    piece = stage_ref[c, off // LANES + j, pl.ds(r, rows, stride=dil), :]
                    o_ref[out_rows, r * w + j * LANES:r * w + (j + 1) * LANES] = piece.astype(BF16)


def _qkv_call(x2d, gn, w, gq, gk, *, cq, ck, outs, tm):
    t = x2d.shape[0]
    c = cq + 2 * ck
    row = lambda r: (r, 0)
    staged = any(d > 1 for _, _, d in outs)
    return pl.pallas_call(
        functools.partial(_qkv_kernel, cq=cq, ck=ck, outs=tuple(outs)),
        grid=(t // tm,),
        in_specs=[
            pl.BlockSpec((tm, D_MODEL), row),
            _const_spec((1, D_MODEL)),
            _const_spec((D_MODEL, c)),
            _const_spec((1, cq)),
            _const_spec((1, ck)),
        ],
        out_specs=[pl.BlockSpec((tm // d, d * w_), row) for _, w_, d in outs],
        out_shape=[jax.ShapeDtypeStruct((t // d, d * w_), BF16) for _, w_, d in outs],
        scratch_shapes=([pltpu.VMEM((tm // min(tm, QKV_CHAIN_ROWS), c // LANES, min(tm, QKV_CHAIN_ROWS), LANES), F32)]
                        if staged else []),
        compiler_params=pltpu.CompilerParams(
            dimension_semantics=("arbitrary",), vmem_limit_bytes=VMEM_LIMIT),
        name="qkv",
    )(x2d, gn, w, gq, gk)


def _attn_kernel(*refs, radius, bq, grp, ncol, has_sink, has_lse, heads_by_col):
    nsub = bq // SUB_ROWS
    wp = SUB_ROWS + 2 * radius
    it = iter(refs)
    fast_ref = next(it)
    hp_ref = next(it)
    q_ref = next(it)
    kp_ref, kc_ref, kn_ref, vp_ref, vc_ref, vn_ref = (next(it) for _ in range(6))
    tab_refs = [next(it) for _ in range(min(nsub, 3))]
    o_ref = next(it)
    lse_ref = next(it) if has_lse else None

    head0 = pl.program_id(0) * (2 * grp) if heads_by_col else 0

    def table(refs, t):
        if t == 0:
            return refs[0]
        return refs[-1] if t == nsub - 1 else refs[1]

    def body(fast):
        for col in range(ncol):
            kcols = slice(col * LANES, (col + 1) * LANES)
            kwin = jnp.concatenate([kp_ref[0, :, kcols], kc_ref[0, :, kcols], kn_ref[0, :, kcols]], axis=0)
            vwin = jnp.concatenate([vp_ref[0, :, kcols], vc_ref[0, :, kcols], vn_ref[0, :, kcols]], axis=0)
            klane = lax.broadcasted_iota(jnp.int32, kwin.shape, 1)
            kzero = jnp.zeros_like(kwin)
            k_half = [jnp.where(klane < HEAD_DIM, kwin, kzero),
                      jnp.where(klane >= HEAD_DIM, kwin, kzero)]
            (bound_softmax_column if fast else rowmax_softmax_column)(col, k_half, vwin)

    def stacked_q(col, t):
        q0 = col * grp * LANES
        return jnp.concatenate(
            [q_ref[0, t * SUB_ROWS:(t + 1) * SUB_ROWS, q0 + a * LANES:q0 + (a + 1) * LANES]
             for a in range(grp)], axis=0)

    def bound_softmax_column(col, k_half, vwin):
        kcols = slice(col * LANES, (col + 1) * LANES)
        q0 = col * grp * LANES
        g_rows = grp * SUB_ROWS
        olane = lax.broadcasted_iota(jnp.int32, (SUB_ROWS, LANES), 1)
        ones = jnp.ones((wp, LANES), BF16)
        for t in range(nsub):
            r0 = t * SUB_ROWS
            qst = stacked_q(col, t)
            probs = []
            for par in range(2):
                sc = lax.dot_general(qst, k_half[par][r0:r0 + wp], (((1,), (1,)), ((), ())),
                                     preferred_element_type=F32)
                probs.append(jnp.exp2(sc + table(tab_refs, t)[par * g_rows:(par + 1) * g_rows, :]).astype(BF16))
            pv = jnp.dot(jnp.concatenate(probs, axis=0), jnp.concatenate([vwin[r0:r0 + wp], ones], axis=1),
                         preferred_element_type=F32)
            lse_acc = jnp.zeros((SUB_ROWS, LANES), F32) if has_lse else None
            for a in range(grp):
                halves = []
                for par in range(2):
                    h = par * grp + a
                    rows = slice(h * SUB_ROWS, (h + 1) * SUB_ROWS)
                    den = pv[rows, LANES:]
                    if has_sink:
                        den = den + hp_ref[2, head0 + h]
                    halves.append(pv[rows, :LANES] / den)
                    if has_lse:
                        lse_acc = jnp.where(olane == h, LN2 * hp_ref[0, head0 + h] + jnp.log(den), lse_acc)
                o_ref[0, r0:r0 + SUB_ROWS, q0 + a * LANES:q0 + (a + 1) * LANES] = jnp.where(
                    olane < HEAD_DIM, halves[0], halves[1]).astype(BF16)
            if has_lse:
                lse_ref[0, r0:r0 + SUB_ROWS, kcols] = lse_acc

    def rowmax_softmax_column(col, k_half, vwin):
        kcols = slice(col * LANES, (col + 1) * LANES)
        q0 = col * grp * LANES
        olane = lax.broadcasted_iota(jnp.int32, (SUB_ROWS, LANES), 1)
        for t in range(nsub):
            r0 = t * SUB_ROWS
            qst = stacked_q(col, t)
            probs, dens, maxes = [], [], []
            for par in range(2):
                sc = lax.dot_general(qst, k_half[par][r0:r0 + wp], (((1,), (1,)), ((), ())),
                                     preferred_element_type=F32)
                for a in range(grp):
                    h = par * grp + a
                    sa = sc[a * SUB_ROWS:(a + 1) * SUB_ROWS] + table(tab_refs, t)[h * SUB_ROWS:(h + 1) * SUB_ROWS, :]
                    m = jnp.max(sa, axis=-1, keepdims=True)
                    if has_sink:
                        sink = hp_ref[1, head0 + h]
                        m = jnp.maximum(m, sink)
                    p = jnp.exp2(sa - m)
                    den = jnp.sum(p, axis=-1, keepdims=True)
                    if has_sink:
                        den = den + jnp.exp2(sink - m)
                    probs.append(p.astype(BF16))
                    dens.append(den)
                    maxes.append(m)
            pv = jnp.dot(jnp.concatenate(probs, axis=0), vwin[r0:r0 + wp], preferred_element_type=F32)
            lse_acc = jnp.zeros((SUB_ROWS, LANES), F32) if has_lse else None
            for a in range(grp):
                halves = []
                for par in range(2):
                    h = par * grp + a
                    halves.append(pv[h * SUB_ROWS:(h + 1) * SUB_ROWS] / dens[h])
                    if has_lse:
                        lse_acc = jnp.where(olane == h, LN2 * maxes[h] + jnp.log(dens[h]), lse_acc)
                o_ref[0, r0:r0 + SUB_ROWS, q0 + a * LANES:q0 + (a + 1) * LANES] = jnp.where(
                    olane < HEAD_DIM, halves[0], halves[1]).astype(BF16)
            if has_lse:
                lse_ref[0, r0:r0 + SUB_ROWS, kcols] = lse_acc

    @pl.when(fast_ref[0] != 0)
    def _():
        body(True)

    @pl.when(fast_ref[0] == 0)
    def _():
        body(False)


def _bias_tables(radius, stride, slopes, stab, grp):
    wp = SUB_ROWS + 2 * radius
    i = np.arange(SUB_ROWS)[:, None]
    j = np.arange(wp)[None, :]
    rel = j - radius - i
    band = np.abs(rel) <= radius
    tabs = []
    for code in range(4):
        valid = band
        if code & 1:
            valid = valid & (j >= radius)
        if code & 2:
            valid = valid & (j < SUB_ROWS + radius)
        tabs.append(np.where(valid, -float(stride) * LOG2E * np.abs(rel), MASK_BIAS))
    base = jnp.asarray(np.stack(tabs), dtype=F32)
    slabs = slopes.shape[0]
    full = base[:, None, None] * slopes[None, :, :, None, None] - stab[None, :, :, None, None]
    return full.reshape(4, slabs, 2 * grp * SUB_ROWS, wp)


def _softmax_params(q_gain_scaled, k_gain, sink2, n_heads):
    bound = HEAD_DIM * jnp.max(jnp.abs(q_gain_scaled)) * jnp.max(jnp.abs(k_gain))
    fast = bound <= FAST_SOFTMAX_MAX_BOUND
    sink2 = jnp.zeros((n_heads,), F32) if sink2 is None else sink2
    stab = jnp.where(fast, jnp.maximum(bound, sink2), 0.0)
    hp = jnp.stack([stab, sink2 - stab, jnp.exp2(sink2 - stab)])
    return fast.astype(jnp.int32).reshape(1), hp


def _attn_call(q, k, v, slopes, fast, hp, *, radius, stride, grp, has_sink, want_lse):
    n, length, _ = q.shape
    cols = k.shape[2] // LANES
    heads_by_col = has_sink
    bq = next(c for c in (ATTN_STEP_ROWS, 512, 256, SUB_ROWS) if length % c == 0)
    ncol = 1 if heads_by_col else math.gcd(cols, max(1, ATTN_STEP_ROWS // bq))
    assert bq % radius == 0
    nsub = bq // SUB_ROWS
    wp = SUB_ROWS + 2 * radius
    g_rows = 2 * grp * SUB_ROWS
    nblk = length // bq
    halo_per_blk = bq // radius
    n_halo = length // radius
    tables = _bias_tables(radius, stride, slopes, hp[0].reshape(slopes.shape), grp)

    def prev_map(c, b, i):
        return (b, jnp.maximum(i * halo_per_blk - 1, 0), c)

    def next_map(c, b, i):
        return (b, jnp.minimum((i + 1) * halo_per_blk, n_halo - 1), c)

    cur_map = lambda c, b, i: (b, i, c)

    def tab_map(first, last):
        def f(c, b, i):
            code = jnp.int32(0)
            if first:
                code = code + (i == 0).astype(jnp.int32)
            if last:
                code = code + 2 * (i == nblk - 1).astype(jnp.int32)
            return (code, c if heads_by_col else 0, 0, 0)
        return f

    tab_maps = [tab_map(True, True)] if nsub == 1 else (
        [tab_map(True, False)] + [tab_map(False, False)] * (nsub > 2) + [tab_map(False, True)])

    smem = pl.BlockSpec(memory_space=pltpu.SMEM)
    in_specs = [smem, smem, pl.BlockSpec((1, bq, ncol * grp * LANES), cur_map)]
    args = [fast, hp, q]
    for arr in (k, v):
        in_specs += [pl.BlockSpec((1, radius, ncol * LANES), prev_map),
                     pl.BlockSpec((1, bq, ncol * LANES), cur_map),
                     pl.BlockSpec((1, radius, ncol * LANES), next_map)]
        args += [arr, arr, arr]
    for index_map in tab_maps:
        in_specs.append(pl.BlockSpec((None, None, g_rows, wp), index_map))
        args.append(tables)

    out_specs = [pl.BlockSpec((1, bq, ncol * grp * LANES), cur_map)]
    out_shape = [jax.ShapeDtypeStruct(q.shape, BF16)]
    if want_lse:
        out_specs.append(pl.BlockSpec((1, bq, ncol * LANES), cur_map))
        out_shape.append(jax.ShapeDtypeStruct((n, length, cols * LANES), F32))

    return pl.pallas_call(
        functools.partial(_attn_kernel, radius=radius, bq=bq, grp=grp, ncol=ncol, has_sink=has_sink,
                          has_lse=want_lse, heads_by_col=heads_by_col),
        grid=(cols // ncol, n, nblk),
        in_specs=in_specs,
        out_specs=out_specs,
        out_shape=out_shape,
        compiler_params=pltpu.CompilerParams(
            dimension_semantics=("arbitrary", "arbitrary", "arbitrary"),
            vmem_limit_bytes=VMEM_LIMIT),
        name="attn",
    )(*args)


def _ffn_chunks():
    chunks, c = [], 0
    while c < FFN_HIDDEN:
        w = min(MXU_DIM, FFN_HIDDEN - c)
        chunks.append((c, w))
        c += w
    return chunks


def _gather_residues(src_ref, stage_ref, dil):
    if dil == 1:
        return src_ref[...].astype(F32)
    slabs, rows, _ = stage_ref.shape
    w = slabs * LANES
    for r in range(dil):
        for j in range(slabs):
            piece = src_ref[:, r * w + j * LANES:r * w + (j + 1) * LANES]
            stage_ref[j, pl.ds(r, rows // dil, stride=dil), :] = piece.astype(F32)
    parts = [stage_ref[j] for j in range(slabs)]
    return parts[0] if slabs == 1 else jnp.concatenate(parts, axis=1)


def _post_kernel(*refs, dils):
    n_groups = max(len(dils), 1)
    it = iter(refs)
    x_ref = next(it)
    o_refs = [next(it) for _ in range(n_groups)]
    lse_refs = [next(it) for _ in dils]
    expand_ref = next(it) if dils else None
    p_ref = next(it)
    wo_ref, nf_ref, wg_ref, wu_ref, wd_ref, np_ref, pg_ref, pp_ref = (next(it) for _ in range(8))
    out_ref = next(it)
    o_stage = {g: next(it) for g, d in enumerate(dils) if d > 1}
    l_stage = {g: next(it) for g, d in enumerate(dils) if d > 1}
    act_ref = next(it)

    if not dils:
        o = o_refs[0][...]
    else:
        os_ = [_gather_residues(o_refs[g], o_stage.get(g), d) for g, d in enumerate(dils)]
        lses = [_gather_residues(lse_refs[g], l_stage.get(g), d) for g, d in enumerate(dils)]
        mx = functools.reduce(jnp.maximum, lses)
        es = [jnp.exp(l - mx) for l in lses]
        inv = 1.0 / functools.reduce(lambda a, b: a + b, es)
        expand = expand_ref[...]
        pieces = []
        for g in range(n_groups):
            alpha = (es[g] * inv).astype(BF16)
            spread = jnp.dot(alpha, expand, preferred_element_type=F32)
            pieces.append((os_[g] * spread).astype(BF16))
        o = jnp.concatenate(pieces, axis=1)

    x1 = x_ref[...] + jnp.dot(o, wo_ref[...], preferred_element_type=F32)
    h = _rmsnorm_rows(x1, nf_ref[...]).astype(BF16)
    for c, w in _ffn_chunks():
        g = jnp.dot(h, wg_ref[:, c:c + w], preferred_element_type=F32)
        u = jnp.dot(h, wu_ref[:, c:c + w], preferred_element_type=F32)
        act_ref[:, c:c + w] = (g * jax.nn.sigmoid(g) * u).astype(BF16)
    x2 = x1 + jnp.dot(act_ref[...], wd_ref[...], preferred_element_type=F32)
    hp = _rmsnorm_rows(x2, np_ref[...]).astype(BF16)
    gate = jax.nn.sigmoid(jnp.dot(hp, pg_ref[...], preferred_element_type=F32))
    proj = jnp.dot(p_ref[...].astype(BF16), pp_ref[...], preferred_element_type=F32)
    out_ref[...] = x2 + gate * proj


def _head_expand_matrix():
    c = np.arange(B_GRP * LANES)
    head = ((c % LANES) // HEAD_DIM) * B_GRP + c // LANES
    return jnp.asarray((np.arange(LANES)[:, None] == head[None, :]).astype(np.float32), dtype=BF16)


def _post_call(x2d, os_, lses, dils, p3d, layer, wo, nf, wg, wu, wd, npl, pg, pp, *, tm):
    t = x2d.shape[0]
    row = lambda r: (r, 0)
    group_dils = dils if dils else (1,)
    in_specs = [pl.BlockSpec((tm, D_MODEL), row)]
    in_specs += [pl.BlockSpec((tm // d, o.shape[1]), row) for o, d in zip(os_, group_dils)]
    in_specs += [pl.BlockSpec((tm // d, l.shape[1]), row) for l, d in zip(lses, dils)]
    extra = []
    if dils:
        extra.append(_head_expand_matrix())
        in_specs.append(_const_spec(extra[0].shape))
    in_specs.append(pl.BlockSpec((None, tm, PLE_DIM), lambda r: (layer, r, 0)))
    stacks = [nf, wg, wu, wd, npl, pg, pp]
    in_specs.append(_const_spec(wo.shape))
    in_specs += [_layer_spec(w.shape, layer) for w in stacks]
    scratch = [pltpu.VMEM((B_GRP, tm, LANES), F32) for d in dils if d > 1]
    scratch += [pltpu.VMEM((1, tm, LANES), F32) for d in dils if d > 1]
    scratch.append(pltpu.VMEM((tm, FFN_HIDDEN), BF16))
    return pl.pallas_call(
        functools.partial(_post_kernel, dils=tuple(dils)),
        grid=(t // tm,),
        in_specs=in_specs,
        out_specs=pl.BlockSpec((tm, D_MODEL), row),
        out_shape=jax.ShapeDtypeStruct((t, D_MODEL), F32),
        scratch_shapes=scratch,
        compiler_params=pltpu.CompilerParams(
            dimension_semantics=("arbitrary",), vmem_limit_bytes=VMEM_LIMIT),
        name="post",
    )(x2d, *os_, *lses, *extra, p3d, wo, *stacks)


def _alibi_slopes(n):
    return 2.0 ** (-8.0 * np.arange(1, n + 1, dtype=np.float64) / n)


def _a_head_order():
    return [(2 * s + par) * A_GRP + a
            for s in range(A_KV_HEADS // 2) for a in range(A_GRP) for par in range(2)]


def _b_head_order():
    return [g * B_Q_PER_GROUP + par * B_GRP + a
            for g in range(len(B_GROUPS)) for a in range(B_GRP) for par in range(2)]


def _slab_param_order(n_slabs, grp):
    return [(s * grp + a) * 2 + par for s in range(n_slabs) for par in range(2) for a in range(grp)]


def _q_gain_scaled(q_gain):
    return q_gain * (HEAD_DIM ** -0.5 * LOG2E)


def _prep_qkv_weights(wqkv, wo, q_gain, k_gain, hq, hkv, order):
    cq, ck = hq * HEAD_DIM, hkv * HEAD_DIM
    w16, wo16 = wqkv.astype(BF16), wo.astype(BF16)
    head = lambda h: slice(h * HEAD_DIM, (h + 1) * HEAD_DIM)
    w = jnp.concatenate([w16[:, head(h)] for h in order] + [w16[:, cq:]], axis=1)
    wo_p = jnp.concatenate([wo16[head(h)] for h in order], axis=0)
    root = HEAD_DIM ** 0.5
    gq = jnp.tile(_q_gain_scaled(q_gain) * root, hq).reshape(1, cq)
    gk = jnp.tile(k_gain * root, hkv).reshape(1, ck)
    return w, wo_p, gq, gk


def _trunk(x, p, norm_mix, a_prep, b_prep, ffn_prep, *, tm):
    b, s, _ = x.shape
    t = b * s
    x2d = x.reshape(t, D_MODEL)
    p3d = p.reshape(p.shape[0], t, PLE_DIM)

    w, wo_p, gq, gk, slopes, fast, hp = a_prep
    cq, ck = A_Q_HEADS * HEAD_DIM, A_KV_HEADS * HEAD_DIM
    q, k, v = _qkv_call(x2d, norm_mix[0].reshape(1, -1), w, gq, gk, cq=cq, ck=ck,
                        outs=[(0, cq, 1), (cq, ck, 1), (cq + ck, ck, 1)], tm=4 * tm)
    o = _attn_call(q.reshape(b, s, cq), k.reshape(b, s, ck), v.reshape(b, s, ck), slopes, fast, hp,
                   radius=A_RADIUS, stride=1, grp=A_GRP, has_sink=True, want_lse=False)[0]
    x2d = _post_call(x2d, [o.reshape(t, cq)], [], (), p3d, 0, wo_p, *ffn_prep, tm=tm)

    w, wo_p, gq, gk, slopes, fast, hp = b_prep
    cq, ck = B_Q_HEADS * HEAD_DIM, B_KV_HEADS * HEAD_DIM
    gq_w = B_GRP * LANES
    dils = tuple(d for _, d in B_GROUPS)
    outs = []
    for g, d in enumerate(dils):
        outs += [(g * gq_w, gq_w, d), (cq + g * LANES, LANES, d), (cq + ck + g * LANES, LANES, d)]
    qkv = _qkv_call(x2d, norm_mix[1].reshape(1, -1), w, gq, gk, cq=cq, ck=ck, outs=outs, tm=2 * tm)
    os_, lses = [], []
    for g, (window, d) in enumerate(B_GROUPS):
        radius = window // (2 * d)
        qg, kg, vg = (a.reshape(b, s // d, a.shape[1]) for a in qkv[3 * g:3 * g + 3])
        og, lg = _attn_call(qg, kg, vg, slopes[g:g + 1], fast, hp[:, g * B_Q_PER_GROUP:(g + 1) * B_Q_PER_GROUP],
                            radius=radius, stride=d, grp=B_GRP, has_sink=False, want_lse=True)
        os_.append(og.reshape(t // d, d * gq_w))
        lses.append(lg.reshape(t // d, d * LANES))
    x2d = _post_call(x2d, os_, lses, dils, p3d, 1, wo_p, *ffn_prep, tm=tm)
    return x2d.reshape(b, s, D_MODEL)


def kernel(x_prompt, x_sample, p_prompt, p_sample, norm_mix, norm_ffn, norm_ple,
           a_wqkv, a_wo, a_q_gain, a_k_gain, a_sink,
           b_wqkv, b_wo, b_q_gain, b_k_gain,
           ffn_w_gate, ffn_w_up, ffn_w_down, ple_w_gate, ple_w_proj):
    a_order, b_order = _a_head_order(), _b_head_order()
    a_slabs = A_KV_HEADS // 2

    a_par = np.asarray(a_order)[_slab_param_order(a_slabs, A_GRP)]
    a_slopes = jnp.asarray(_alibi_slopes(A_Q_HEADS)[a_par].reshape(a_slabs, 2 * A_GRP), dtype=F32)
    a_prep = _prep_qkv_weights(a_wqkv[0], a_wo[0], a_q_gain[0], a_k_gain[0], A_Q_HEADS, A_KV_HEADS, a_order)
    a_prep += (a_slopes,) + _softmax_params(_q_gain_scaled(a_q_gain[0]), a_k_gain[0], a_sink[0][a_par] * LOG2E,
                                            A_Q_HEADS)

    b_par = np.asarray(b_order)[_slab_param_order(len(B_GROUPS), B_GRP)]
    b_slopes = jnp.asarray(_alibi_slopes(B_Q_HEADS)[b_par].reshape(len(B_GROUPS), 2 * B_GRP), dtype=F32)
    b_prep = _prep_qkv_weights(b_wqkv[0], b_wo[0], b_q_gain[0], b_k_gain[0], B_Q_HEADS, B_KV_HEADS, b_order)
    b_prep += (b_slopes,) + _softmax_params(_q_gain_scaled(b_q_gain[0]), b_k_gain[0], None, B_Q_HEADS)

    ffn_prep = (norm_ffn[:, None, :], ffn_w_gate.astype(BF16), ffn_w_up.astype(BF16), ffn_w_down.astype(BF16),
                norm_ple[:, None, :], ple_w_gate.astype(BF16), ple_w_proj.astype(BF16))

    run = functools.partial(_trunk, norm_mix=norm_mix, a_prep=a_prep, b_prep=b_prep,
                            ffn_prep=ffn_prep, tm=512)
    return (run(x_prompt, p_prompt), run(x_sample, p_sample))
```

```python
import functools
import math

import numpy as np
import jax
import jax.numpy as jnp
from jax import lax
from jax.experimental import pallas as pl
from jax.experimental.pallas import tpu as pltpu

D_MODEL = 1024
HEAD_DIM = 64
PLE_DIM = 256
FFN_HIDDEN = 2816
EPS = 1e-6

A_Q_HEADS = 16
A_KV_HEADS = 4
A_RADIUS = 128
B_GROUPS = ((128, 1), (512, 4), (2048, 16))
B_Q_PER_GROUP = 6
B_KV_PER_GROUP = 2
B_Q_HEADS = B_Q_PER_GROUP * len(B_GROUPS)
B_KV_HEADS = B_KV_PER_GROUP * len(B_GROUPS)
A_GRP = A_Q_HEADS // A_KV_HEADS
B_GRP = B_Q_PER_GROUP // B_KV_PER_GROUP

LANES = 128
MXU_DIM = 256
SUB_ROWS = 128
QKV_CHAIN_ROWS = 512
ATTN_STEP_ROWS = 2048
MASK_BIAS = -1e32
VMEM_LIMIT = 56 * 1024 * 1024
LOG2E = math.log2(math.e)
LN2 = math.log(2.0)
FAST_SOFTMAX_MAX_BOUND = 30.0 * LOG2E

F32 = jnp.float32
BF16 = jnp.bfloat16


def _const_spec(shape):
    nd = len(shape)
    return pl.BlockSpec(shape, lambda *_: (0,) * nd, pipeline_mode=pl.Buffered(1))


def _layer_spec(shape, layer):
    return pl.BlockSpec((None,) + tuple(shape[1:]), lambda *_: (layer, 0, 0), pipeline_mode=pl.Buffered(1))


def _rmsnorm_rows(x, gain):
    ms = jnp.mean(x * x, axis=-1, keepdims=True)
    return x * lax.rsqrt(ms + EPS) * gain


def _head_sumsq(t):
    width = t.shape[1]
    lane = lax.broadcasted_iota(jnp.int32, (t.shape[0], LANES), 1)
    parts = []
    for c in range(0, width, LANES):
        blk = t[:, c:c + LANES]
        sq = blk * blk
        tot = jnp.sum(sq, axis=-1, keepdims=True)
        lo = jnp.sum(jnp.where(lane < HEAD_DIM, sq, 0.0), axis=-1, keepdims=True)
        parts.append(jnp.where(lane < HEAD_DIM, lo, tot - lo))
    return parts[0] if len(parts) == 1 else jnp.concatenate(parts, axis=1)


def _normed_qkv(x, gn_ref, w_ref, gq_ref, gk_ref, cq, ck):
    h = _rmsnorm_rows(x, gn_ref[...]).astype(BF16)
    qkv = jnp.dot(h, w_ref[...], preferred_element_type=F32)
    q = qkv[:, :cq]
    k = qkv[:, cq:cq + ck]
    q = q * lax.rsqrt(_head_sumsq(q) + HEAD_DIM * EPS) * gq_ref[...]
    k = k * lax.rsqrt(_head_sumsq(k) + HEAD_DIM * EPS) * gk_ref[...]
    return jnp.concatenate([q, k, qkv[:, cq + ck:]], axis=1)


def _qkv_kernel(x_ref, gn_ref, w_ref, gq_ref, gk_ref, *rest, cq, ck, outs):
    out_refs = rest[:len(outs)]
    stage_ref = rest[len(outs)] if len(rest) > len(outs) else None
    tm = x_ref.shape[0]
    cr = min(tm, QKV_CHAIN_ROWS)
    for c in range(tm // cr):
        _qkv_rows(x_ref[c * cr:(c + 1) * cr, :], c, (gn_ref, w_ref, gq_ref, gk_ref), cq, ck, outs, out_refs,
                  stage_ref)


def _qkv_rows(x, c, params, cq, ck, outs, out_refs, stage_ref):
    cr = x.shape[0]
    full = _normed_qkv(x, *params, cq, ck)
    if stage_ref is not None:
        for j in range(stage_ref.shape[1]):
            stage_ref[c, j] = full[:, j * LANES:(j + 1) * LANES]
    for (off, w, dil), o_ref in zip(outs, out_refs):
        rows = cr // dil
        out_rows = slice(c * rows, (c + 1) * rows)
        if dil == 1:
            o_ref[out_rows, :] = full[:, off:off + w].astype(BF16)
            continue
        for r in range(dil):
            for j in range(w // LANES):
                piece = stage_ref[c, off // LANES + j, pl.ds(r, rows, stride=dil), :]
                o_ref[out_rows, r * w + j * LANES:r * w + (j + 1) * LANES] = piece.astype(BF16)


def _qkv_call(x2d, gn, w, gq, gk, *, cq, ck, outs, tm):
    t = x2d.shape[0]
    c = cq + 2 * ck
    row = lambda r: (r, 0)
    staged = any(d > 1 for _, _, d in outs)
    return pl.pallas_call(
        functools.partial(_qkv_kernel, cq=cq, ck=ck, outs=tuple(outs)),
        grid=(t // tm,),
        in_specs=[
            pl.BlockSpec((tm, D_MODEL), row),
            _const_spec((1, D_MODEL)),
            _const_spec((D_MODEL, c)),
            _const_spec((1, cq)),
            _const_spec((1, ck)),
        ],
        out_specs=[pl.BlockSpec((tm // d, d * w_), row) for _, w_, d in outs],
        out_shape=[jax.ShapeDtypeStruct((t // d, d * w_), BF16) for _, w_, d in outs],
        scratch_shapes=([pltpu.VMEM((tm // min(tm, QKV_CHAIN_ROWS), c // LANES, min(tm, QKV_CHAIN_ROWS), LANES), F32)]
                        if staged else []),
        compiler_params=pltpu.CompilerParams(
            dimension_semantics=("arbitrary",), vmem_limit_bytes=VMEM_LIMIT),
        name="qkv",
    )(x2d, gn, w, gq, gk)


def _attn_kernel(*refs, radius, bq, grp, ncol, has_sink, has_lse, heads_by_col):
    nsub = bq // SUB_ROWS
    wp = SUB_ROWS + 2 * radius
    it = iter(refs)
    fast_ref = next(it)
    hp_ref = next(it)
    q_ref = next(it)
    kp_ref, kc_ref, kn_ref, vp_ref, vc_ref, vn_ref = (next(it) for _ in range(6))
    tab_refs = [next(it) for _ in range(min(nsub, 3))]
    o_ref = next(it)
    lse_ref = next(it) if has_lse else None

    head0 = pl.program_id(0) * (2 * grp) if heads_by_col else 0

    def table(refs, t):
        if t == 0:
            return refs[0]
        return refs[-1] if t == nsub - 1 else refs[1]

    def body(fast):
        for col in range(ncol):
            kcols = slice(col * LANES, (col + 1) * LANES)
            kwin = jnp.concatenate([kp_ref[0, :, kcols], kc_ref[0, :, kcols], kn_ref[0, :, kcols]], axis=0)
            vwin = jnp.concatenate([vp_ref[0, :, kcols], vc_ref[0, :, kcols], vn_ref[0, :, kcols]], axis=0)
            klane = lax.broadcasted_iota(jnp.int32, kwin.shape, 1)
            kzero = jnp.zeros_like(kwin)
            k_half = [jnp.where(klane < HEAD_DIM, kwin, kzero),
                      jnp.where(klane >= HEAD_DIM, kwin, kzero)]
            (bound_softmax_column if fast else rowmax_softmax_column)(col, k_half, vwin)

    def stacked_q(col, t):
        q0 = col * grp * LANES
        return jnp.concatenate(
            [q_ref[0, t * SUB_ROWS:(t + 1) * SUB_ROWS, q0 + a * LANES:q0 + (a + 1) * LANES]
             for a in range(grp)], axis=0)

    def bound_softmax_column(col, k_half, vwin):
        kcols = slice(col * LANES, (col + 1) * LANES)
        q0 = col * grp * LANES
        g_rows = grp * SUB_ROWS
        olane = lax.broadcasted_iota(jnp.int32, (SUB_ROWS, LANES), 1)
        ones = jnp.ones((wp, LANES), BF16)
        for t in range(nsub):
            r0 = t * SUB_ROWS
            qst = stacked_q(col, t)
            probs = []
            for par in range(2):
                sc = lax.dot_general(qst, k_half[par][r0:r0 + wp], (((1,), (1,)), ((), ())),
                                     preferred_element_type=F32)
                probs.append(jnp.exp2(sc + table(tab_refs, t)[par * g_rows:(par + 1) * g_rows, :]).astype(BF16))
            pv = jnp.dot(jnp.concatenate(probs, axis=0), jnp.concatenate([vwin[r0:r0 + wp], ones], axis=1),
                         preferred_element_type=F32)
            lse_acc = jnp.zeros((SUB_ROWS, LANES), F32) if has_lse else None
            for a in range(grp):
                halves = []
                for par in range(2):
                    h = par * grp + a
                    rows = slice(h * SUB_ROWS, (h + 1) * SUB_ROWS)
                    den = pv[rows, LANES:]
                    if has_sink:
                        den = den + hp_ref[2, head0 + h]
                    halves.append(pv[rows, :LANES] / den)
                    if has_lse:
                        lse_acc = jnp.where(olane == h, LN2 * hp_ref[0, head0 + h] + jnp.log(den), lse_acc)
                o_ref[0, r0:r0 + SUB_ROWS, q0 + a * LANES:q0 + (a + 1) * LANES] = jnp.where(
                    olane < HEAD_DIM, halves[0], halves[1]).astype(BF16)
            if has_lse:
                lse_ref[0, r0:r0 + SUB_ROWS, kcols] = lse_acc

    def rowmax_softmax_column(col, k_half, vwin):
        kcols = slice(col * LANES, (col + 1) * LANES)
        q0 = col * grp * LANES
        olane = lax.broadcasted_iota(jnp.int32, (SUB_ROWS, LANES), 1)
        for t in range(nsub):
            r0 = t * SUB_ROWS
            qst = stacked_q(col, t)
            probs, dens, maxes = [], [], []
            for par in range(2):
                sc = lax.dot_general(qst, k_half[par][r0:r0 + wp], (((1,), (1,)), ((), ())),
                                     preferred_element_type=F32)
                for a in range(grp):
                    h = par * grp + a
                    sa = sc[a * SUB_ROWS:(a + 1) * SUB_ROWS] + table(tab_refs, t)[h * SUB_ROWS:(h + 1) * SUB_ROWS, :]
                    m = jnp.max(sa, axis=-1, keepdims=True)
                    if has_sink:
                        sink = hp_ref[1, head0 + h]
                        m = jnp.maximum(m, sink)
                    p = jnp.exp2(sa - m)
                    den = jnp.sum(p, axis=-1, keepdims=True)
                    if has_sink:
                        den = den + jnp.exp2(sink - m)
                    probs.append(p.astype(BF16))
                    dens.append(den)
                    maxes.append(m)
            pv = jnp.dot(jnp.concatenate(probs, axis=0), vwin[r0:r0 + wp], preferred_element_type=F32)
            lse_acc = jnp.zeros((SUB_ROWS, LANES), F32) if has_lse else None
            for a in range(grp):
                halves = []
                for par in range(2):
                    h = par * grp + a
                    halves.append(pv[h * SUB_ROWS:(h + 1) * SUB_ROWS] / dens[h])
                    if has_lse:
                        lse_acc = jnp.where(olane == h, LN2 * maxes[h] + jnp.log(dens[h]), lse_acc)
                o_ref[0, r0:r0 + SUB_ROWS, q0 + a * LANES:q0 + (a + 1) * LANES] = jnp.where(
                    olane < HEAD_DIM, halves[0], halves[1]).astype(BF16)
            if has_lse:
                lse_ref[0, r0:r0 + SUB_ROWS, kcols] = lse_acc

    @pl.when(fast_ref[0] != 0)
    def _():
        body(True)

    @pl.when(fast_ref[0] == 0)
    def _():
        body(False)


def _bias_tables(radius, stride, slopes, stab, grp):
    wp = SUB_ROWS + 2 * radius
    i = np.arange(SUB_ROWS)[:, None]
    j = np.arange(wp)[None, :]
    rel = j - radius - i
    band = np.abs(rel) <= radius
    tabs = []
    for code in range(4):
        valid = band
        if code & 1:
            valid = valid & (j >= radius)
        if code & 2:
            valid = valid & (j < SUB_ROWS + radius)
        tabs.append(np.where(valid, -float(stride) * LOG2E * np.abs(rel), MASK_BIAS))
    base = jnp.asarray(np.stack(tabs), dtype=F32)
    slabs = slopes.shape[0]
    full = base[:, None, None] * slopes[None, :, :, None, None] - stab[None, :, :, None, None]
    return full.reshape(4, slabs, 2 * grp * SUB_ROWS, wp)


def _softmax_params(q_gain_scaled, k_gain, sink2, n_heads):
    bound = HEAD_DIM * jnp.max(jnp.abs(q_gain_scaled)) * jnp.max(jnp.abs(k_gain))
    fast = bound <= FAST_SOFTMAX_MAX_BOUND
    sink2 = jnp.zeros((n_heads,), F32) if sink2 is None else sink2
    stab = jnp.where(fast, jnp.maximum(bound, sink2), 0.0)
    hp = jnp.stack([stab, sink2 - stab, jnp.exp2(sink2 - stab)])
    return fast.astype(jnp.int32).reshape(1), hp


def _attn_call(q, k, v, slopes, fast, hp, *, radius, stride, grp, has_sink, want_lse):
    n, length, _ = q.shape
    cols = k.shape[2] // LANES
    heads_by_col = has_sink
    bq = next(c for c in (ATTN_STEP_ROWS, 512, 256, SUB_ROWS) if length % c == 0)
    ncol = 1 if heads_by_col else math.gcd(cols, max(1, ATTN_STEP_ROWS // bq))
    assert bq % radius == 0
    nsub = bq // SUB_ROWS
    wp = SUB_ROWS + 2 * radius
    g_rows = 2 * grp * SUB_ROWS
    nblk = length // bq
    halo_per_blk = bq // radius
    n_halo = length // radius
    tables = _bias_tables(radius, stride, slopes, hp[0].reshape(slopes.shape), grp)

    def prev_map(c, b, i):
        return (b, jnp.maximum(i * halo_per_blk - 1, 0), c)

    def next_map(c, b, i):
        return (b, jnp.minimum((i + 1) * halo_per_blk, n_halo - 1), c)

    cur_map = lambda c, b, i: (b, i, c)

    def tab_map(first, last):
        def f(c, b, i):
            code = jnp.int32(0)
            if first:
                code = code + (i == 0).astype(jnp.int32)
            if last:
                code = code + 2 * (i == nblk - 1).astype(jnp.int32)
            return (code, c if heads_by_col else 0, 0, 0)
        return f

    tab_maps = [tab_map(True, True)] if nsub == 1 else (
        [tab_map(True, False)] + [tab_map(False, False)] * (nsub > 2) + [tab_map(False, True)])

    smem = pl.BlockSpec(memory_space=pltpu.SMEM)
    in_specs = [smem, smem, pl.BlockSpec((1, bq, ncol * grp * LANES), cur_map)]
    args = [fast, hp, q]
    for arr in (k, v):
        in_specs += [pl.BlockSpec((1, radius, ncol * LANES), prev_map),
                     pl.BlockSpec((1, bq, ncol * LANES), cur_map),
                     pl.BlockSpec((1, radius, ncol * LANES), next_map)]
        args += [arr, arr, arr]
    for index_map in tab_maps:
        in_specs.append(pl.BlockSpec((None, None, g_rows, wp), index_map))
        args.append(tables)

    out_specs = [pl.BlockSpec((1, bq, ncol * grp * LANES), cur_map)]
    out_shape = [jax.ShapeDtypeStruct(q.shape, BF16)]
    if want_lse:
        out_specs.append(pl.BlockSpec((1, bq, ncol * LANES), cur_map))
        out_shape.append(jax.ShapeDtypeStruct((n, length, cols * LANES), F32))

    return pl.pallas_call(
        functools.partial(_attn_kernel, radius=radius, bq=bq, grp=grp, ncol=ncol, has_sink=has_sink,
                          has_lse=want_lse, heads_by_col=heads_by_col),
        grid=(cols // ncol, n, nblk),
        in_specs=in_specs,
        out_specs=out_specs,
        out_shape=out_shape,
        compiler_params=pltpu.CompilerParams(
            dimension_semantics=("arbitrary", "arbitrary", "arbitrary"),
            vmem_limit_bytes=VMEM_LIMIT),
        name="attn",
    )(*args)


def _ffn_chunks():
    chunks, c = [], 0
    while c < FFN_HIDDEN:
        w = min(MXU_DIM, FFN_HIDDEN - c)
        chunks.append((c, w))
        c += w
    return chunks


def _gather_residues(src_ref, stage_ref, dil):
    if dil == 1:
        return src_ref[...].astype(F32)
    slabs, rows, _ = stage_ref.shape
    w = slabs * LANES
    for r in range(dil):
        for j in range(slabs):
            piece = src_ref[:, r * w + j * LANES:r * w + (j + 1) * LANES]
            stage_ref[j, pl.ds(r, rows // dil, stride=dil), :] = piece.astype(F32)
    parts = [stage_ref[j] for j in range(slabs)]
    return parts[0] if slabs == 1 else jnp.concatenate(parts, axis=1)


def _post_kernel(*refs, dils, nxt):
    n_groups = max(len(dils), 1)
    it = iter(refs)
    x_ref = next(it)
    o_refs = [next(it) for _ in range(n_groups)]
    lse_refs = [next(it) for _ in dils]
    expand_ref = next(it) if dils else None
    p_ref = next(it)
    wo_ref, nf_ref, wg_ref, wu_ref, wd_ref, np_ref, pg_ref, pp_ref = (next(it) for _ in range(8))
    nxt_params = tuple(next(it) for _ in range(4)) if nxt else None
    out_ref = next(it)
    nxt_out_refs = [next(it) for _ in nxt[2]] if nxt else []
    o_stage = {g: next(it) for g, d in enumerate(dils) if d > 1}
    l_stage = {g: next(it) for g, d in enumerate(dils) if d > 1}
    act_ref = next(it)
    nxt_stage_ref = next(it) if nxt else None

    if not dils:
        o = o_refs[0][...]
    else:
        os_ = [_gather_residues(o_refs[g], o_stage.get(g), d) for g, d in enumerate(dils)]
        lses = [_gather_residues(lse_refs[g], l_stage.get(g), d) for g, d in enumerate(dils)]
        mx = functools.reduce(jnp.maximum, lses)
        es = [jnp.exp(l - mx) for l in lses]
        inv = 1.0 / functools.reduce(lambda a, b: a + b, es)
        expand = expand_ref[...]
        pieces = []
        for g in range(n_groups):
            alpha = (es[g] * inv).astype(BF16)
            spread = jnp.dot(alpha, expand, preferred_element_type=F32)
            pieces.append((os_[g] * spread).astype(BF16))
        o = jnp.concatenate(pieces, axis=1)

    x1 = x_ref[...] + jnp.dot(o, wo_ref[...], preferred_element_type=F32)
    h = _rmsnorm_rows(x1, nf_ref[...]).astype(BF16)
    for c, w in _ffn_chunks():
        g = jnp.dot(h, wg_ref[:, c:c + w], preferred_element_type=F32)
        u = jnp.dot(h, wu_ref[:, c:c + w], preferred_element_type=F32)
        act_ref[:, c:c + w] = (g * jax.nn.sigmoid(g) * u).astype(BF16)
    x2 = x1 + jnp.dot(act_ref[...], wd_ref[...], preferred_element_type=F32)
    hp = _rmsnorm_rows(x2, np_ref[...]).astype(BF16)
    gate = jax.nn.sigmoid(jnp.dot(hp, pg_ref[...], preferred_element_type=F32))
    proj = jnp.dot(p_ref[...].astype(BF16), pp_ref[...], preferred_element_type=F32)
    x3 = x2 + gate * proj
    out_ref[...] = x3
    if nxt:
        _qkv_rows(x3, 0, nxt_params, nxt[0], nxt[1], nxt[2], nxt_out_refs, nxt_stage_ref)


def _head_expand_matrix():
    c = np.arange(B_GRP * LANES)
    head = ((c % LANES) // HEAD_DIM) * B_GRP + c // LANES
    return jnp.asarray((np.arange(LANES)[:, None] == head[None, :]).astype(np.float32), dtype=BF16)


def _post_call(x2d, os_, lses, dils, p3d, layer, wo, nf, wg, wu, wd, npl, pg, pp, *, tm, nxt=None):
    t = x2d.shape[0]
    row = lambda r: (r, 0)
    group_dils = dils if dils else (1,)
    in_specs = [pl.BlockSpec((tm, D_MODEL), row)]
    in_specs += [pl.BlockSpec((tm // d, o.shape[1]), row) for o, d in zip(os_, group_dils)]
    in_specs += [pl.BlockSpec((tm // d, l.shape[1]), row) for l, d in zip(lses, dils)]
    extra = []
    if dils:
        extra.append(_head_expand_matrix())
        in_specs.append(_const_spec(extra[0].shape))
    in_specs.append(pl.BlockSpec((None, tm, PLE_DIM), lambda r: (layer, r, 0)))
    stacks = [nf, wg, wu, wd, npl, pg, pp]
    in_specs.append(_const_spec(wo.shape))
    in_specs += [_layer_spec(w.shape, layer) for w in stacks]
    scratch = [pltpu.VMEM((B_GRP, tm, LANES), F32) for d in dils if d > 1]
    scratch += [pltpu.VMEM((1, tm, LANES), F32) for d in dils if d > 1]
    scratch.append(pltpu.VMEM((tm, FFN_HIDDEN), BF16))
    out_specs = [pl.BlockSpec((tm, D_MODEL), row)]
    out_shape = [jax.ShapeDtypeStruct((t, D_MODEL), F32)]
    nxt_args, nxt_static = [], None
    if nxt is not None:
        nxt_args, (cq, ck, outs) = list(nxt[:4]), nxt[4:]
        nxt_static = (cq, ck, tuple(outs))
        in_specs += [_const_spec(a.shape) for a in nxt_args]
        out_specs += [pl.BlockSpec((tm // d, d * w_), row) for _, w_, d in outs]
        out_shape += [jax.ShapeDtypeStruct((t // d, d * w_), BF16) for _, w_, d in outs]
        scratch.append(pltpu.VMEM((1, (cq + 2 * ck) // LANES, tm, LANES), F32))
    return pl.pallas_call(
        functools.partial(_post_kernel, dils=tuple(dils), nxt=nxt_static),
        grid=(t // tm,),
        in_specs=in_specs,
        out_specs=out_specs,
        out_shape=out_shape,
        scratch_shapes=scratch,
        compiler_params=pltpu.CompilerParams(
            dimension_semantics=("arbitrary",), vmem_limit_bytes=VMEM_LIMIT),
        name="post",
    )(x2d, *os_, *lses, *extra, p3d, wo, *stacks, *nxt_args)


def _alibi_slopes(n):
    return 2.0 ** (-8.0 * np.arange(1, n + 1, dtype=np.float64) / n)


def _a_head_order():
    return [(2 * s + par) * A_GRP + a
            for s in range(A_KV_HEADS // 2) for a in range(A_GRP) for par in range(2)]


def _b_head_order():
    return [g * B_Q_PER_GROUP + par * B_GRP + a
            for g in range(len(B_GROUPS)) for a in range(B_GRP) for par in range(2)]


def _slab_param_order(n_slabs, grp):
    return [(s * grp + a) * 2 + par for s in range(n_slabs) for par in range(2) for a in range(grp)]


def _q_gain_scaled(q_gain):
    return q_gain * (HEAD_DIM ** -0.5 * LOG2E)


def _prep_qkv_weights(wqkv, wo, q_gain, k_gain, hq, hkv, order):
    cq, ck = hq * HEAD_DIM, hkv * HEAD_DIM
    w16, wo16 = wqkv.astype(BF16), wo.astype(BF16)
    head = lambda h: slice(h * HEAD_DIM, (h + 1) * HEAD_DIM)
    w = jnp.concatenate([w16[:, head(h)] for h in order] + [w16[:, cq:]], axis=1)
    wo_p = jnp.concatenate([wo16[head(h)] for h in order], axis=0)
    root = HEAD_DIM ** 0.5
    gq = jnp.tile(_q_gain_scaled(q_gain) * root, hq).reshape(1, cq)
    gk = jnp.tile(k_gain * root, hkv).reshape(1, ck)
    return w, wo_p, gq, gk


def _trunk(x, p, norm_mix, a_prep, b_prep, ffn_prep, *, tm):
    b, s, _ = x.shape
    t = b * s
    x2d = x.reshape(t, D_MODEL)
    p3d = p.reshape(p.shape[0], t, PLE_DIM)

    w, wo_p, gq, gk, slopes, fast, hp = a_prep
    cq, ck = A_Q_HEADS * HEAD_DIM, A_KV_HEADS * HEAD_DIM
    q, k, v = _qkv_call(x2d, norm_mix[0].reshape(1, -1), w, gq, gk, cq=cq, ck=ck,
                        outs=[(0, cq, 1), (cq, ck, 1), (cq + ck, ck, 1)], tm=4 * tm)
    o = _attn_call(q.reshape(b, s, cq), k.reshape(b, s, ck), v.reshape(b, s, ck), slopes, fast, hp,
                   radius=A_RADIUS, stride=1, grp=A_GRP, has_sink=True, want_lse=False)[0]
    w_b, wo_b, gq_b, gk_b, slopes_b, fast_b, hp_b = b_prep
    cq_b, ck_b = B_Q_HEADS * HEAD_DIM, B_KV_HEADS * HEAD_DIM
    gq_w = B_GRP * LANES
    dils = tuple(d for _, d in B_GROUPS)
    outs = []
    for g, d in enumerate(dils):
        outs += [(g * gq_w, gq_w, d), (cq_b + g * LANES, LANES, d), (cq_b + ck_b + g * LANES, LANES, d)]
    x2d, *qkv = _post_call(x2d, [o.reshape(t, cq)], [], (), p3d, 0, wo_p, *ffn_prep, tm=tm,
                           nxt=(norm_mix[1].reshape(1, -1), w_b, gq_b, gk_b, cq_b, ck_b, outs))
    wo_p, slopes, fast, hp = wo_b, slopes_b, fast_b, hp_b
    os_, lses = [], []
    for g, (window, d) in enumerate(B_GROUPS):
        radius = window // (2 * d)
        qg, kg, vg = (a.reshape(b, s // d, a.shape[1]) for a in qkv[3 * g:3 * g + 3])
        og, lg = _attn_call(qg, kg, vg, slopes[g:g + 1], fast, hp[:, g * B_Q_PER_GROUP:(g + 1) * B_Q_PER_GROUP],
                            radius=radius, stride=d, grp=B_GRP, has_sink=False, want_lse=True)
        os_.append(og.reshape(t // d, d * gq_w))
        lses.append(lg.reshape(t // d, d * LANES))
    x2d = _post_call(x2d, os_, lses, dils, p3d, 1, wo_p, *ffn_prep, tm=tm)[0]
    return x2d.reshape(b, s, D_MODEL)


def kernel(x_prompt, x_sample, p_prompt, p_sample, norm_mix, norm_ffn, norm_ple,
           a_wqkv, a_wo, a_q_gain, a_k_gain, a_sink,
           b_wqkv, b_wo, b_q_gain, b_k_gain,
           ffn_w_gate, ffn_w_up, ffn_w_down, ple_w_gate, ple_w_proj):
    a_order, b_order = _a_head_order(), _b_head_order()
    a_slabs = A_KV_HEADS // 2

    a_par = np.asarray(a_order)[_slab_param_order(a_slabs, A_GRP)]
    a_slopes = jnp.asarray(_alibi_slopes(A_Q_HEADS)[a_par].reshape(a_slabs, 2 * A_GRP), dtype=F32)
    a_prep = _prep_qkv_weights(a_wqkv[0], a_wo[0], a_q_gain[0], a_k_gain[0], A_Q_HEADS, A_KV_HEADS, a_order)
    a_prep += (a_slopes,) + _softmax_params(_q_gain_scaled(a_q_gain[0]), a_k_gain[0], a_sink[0][a_par] * LOG2E,
                                            A_Q_HEADS)

    b_par = np.asarray(b_order)[_slab_param_order(len(B_GROUPS), B_GRP)]
    b_slopes = jnp.asarray(_alibi_slopes(B_Q_HEADS)[b_par].reshape(len(B_GROUPS), 2 * B_GRP), dtype=F32)
    b_prep = _prep_qkv_weights(b_wqkv[0], b_wo[0], b_q_gain[0], b_k_gain[0], B_Q_HEADS, B_KV_HEADS, b_order)
    b_prep += (b_slopes,) + _softmax_params(_q_gain_scaled(b_q_gain[0]), b_k_gain[0], None, B_Q_HEADS)

    ffn_prep = (norm_ffn[:, None, :], ffn_w_gate.astype(BF16), ffn_w_up.astype(BF16), ffn_w_down.astype(BF16),
                norm_ple[:, None, :], ple_w_gate.astype(BF16), ple_w_proj.astype(BF16))

    run = functools.partial(_trunk, norm_mix=norm_mix, a_prep=a_prep, b_prep=b_prep,
                            ffn_prep=ffn_prep, tm=512)
    return (run(x_prompt, p_prompt), run(x_sample, p_sample))
```
